```python
import jax
import jax.numpy as jnp
from jax import lax
import numpy as np


D_MODEL = 1024
BATCH = 8
SEQ = 4096
DEPTH = 1

HEAD_DIM = 64
ROPE_THETA = 10000.0
RMS_EPS = 1e-6
NEG_INF = -1e30
Q_BLOCK = 64

A_HEADS = 8
IDX_HEADS = 8
IDX_DIM = 32
DSA_TOPK_MAX = 256

B_HEADS = 8
B_KV_HEADS = 2
B_GROUP = B_HEADS // B_KV_HEADS
CMP_LEN = 32
CMP_STRIDE = 16
CMP_HIDDEN = 128
SLC_LEN = 64
SLC_TOPN = 16
WINDOW = 512

N_GROUPS = 4
EXPERTS_PER_GROUP = 8
N_EXPERTS = N_GROUPS * EXPERTS_PER_GROUP
TOP_K_INNER = 2
D_EXPERT = 256
MOE_BLOCK = 256

IN_SPLITS = (A_HEADS * HEAD_DIM, HEAD_DIM, HEAD_DIM, IDX_HEADS * IDX_DIM, IDX_DIM, IDX_HEADS,
             B_HEADS * HEAD_DIM, 6 * B_KV_HEADS * HEAD_DIM, 3 * B_HEADS, D_MODEL, D_MODEL)
IN_WIDTH = sum(IN_SPLITS)
SPLIT_POINTS = tuple(int(v) for v in np.cumsum(IN_SPLITS)[:-1])

kernel_name = "hybrid_dsa_nsa_hmoe_block"


def rmsnorm(x, g):
    xf = x.astype(jnp.float32)
    y = xf * lax.rsqrt(jnp.mean(xf * xf, axis=-1, keepdims=True) + RMS_EPS)
    return (y * g.astype(jnp.float32)).astype(x.dtype)


def rope(x, pos):
    half = x.shape[-1] // 2
    inv_freq = ROPE_THETA ** (-jnp.arange(half, dtype=jnp.float32) / half)
    ang = pos.astype(jnp.float32)[:, :, None] * inv_freq
    cos = jnp.cos(ang)[:, :, None, :]
    sin = jnp.sin(ang)[:, :, None, :]
    xf = x.astype(jnp.float32)
    x1, x2 = xf[..., :half], xf[..., half:]
    return jnp.concatenate([x1 * cos - x2 * sin, x1 * sin + x2 * cos], axis=-1).astype(x.dtype)


def masked_softmax(s, mask):
    p = jax.nn.softmax(jnp.where(mask, s.astype(jnp.float32), NEG_INF), axis=-1)
    return jnp.where(mask, p, 0.0)


def _rows(a, t0):
    return lax.dynamic_slice_in_dim(a, t0, Q_BLOCK, axis=1)


def dsa_attention(q, k, v, qi, ki, wi):
    bsz, L = q.shape[0], q.shape[1]
    k_sel = min(DSA_TOPK_MAX, L // 4)
    key_pos = jnp.arange(L)
    b_ix = jnp.arange(bsz)[:, None, None]
    scale = HEAD_DIM ** -0.5

    def block(i):
        t0 = i * Q_BLOCK
        qpos = t0 + jnp.arange(Q_BLOCK)
        causal = (key_pos[None, :] <= qpos[:, None])[None]
        idx_logit = jax.nn.relu(jnp.einsum('bthd,bsd->bths', _rows(qi, t0), ki))
        score = jnp.einsum('bth,bths->bts', _rows(wi, t0), idx_logit).astype(jnp.float32)
        score = jnp.where(causal, score, -jnp.inf)
        _, sel = lax.top_k(score, k_sel)
        ks = k[b_ix, sel]
        vs = v[b_ix, sel]
        s = jnp.einsum('bthd,btkd->bthk', _rows(q, t0), ks) * scale
        mask = (sel <= qpos[None, :, None])[:, :, None, :]
        p = masked_softmax(s, mask).astype(vs.dtype)
        return jnp.einsum('bthk,btkd->bthd', p, vs)

    out = lax.map(block, jnp.arange(L // Q_BLOCK))
    return out.transpose(1, 0, 2, 3, 4).reshape(bsz, L, A_HEADS * HEAD_DIM)


def compress_tokens(kv, pe, w1, w2):
    bsz, L, hk, dh = kv.shape
    n_cmp = (L - CMP_LEN) // CMP_STRIDE + 1
    idx = np.arange(n_cmp)[:, None] * CMP_STRIDE + np.arange(CMP_LEN)[None, :]
    blocks = kv[:, idx] + pe[:, None, :]
    z = blocks.transpose(0, 1, 3, 2, 4).reshape(bsz, n_cmp, hk, CMP_LEN * dh)
    return jax.nn.silu(z @ w1) @ w2


def nsa_attention(q_nope, q_rope, k_cmp, v_cmp, k_slc, v_slc, k_win, v_win, gates,
                  pe_k, w1_k, w2_k, pe_v, w1_v, w2_v):
    bsz, L = q_nope.shape[0], q_nope.shape[1]
    scale = HEAD_DIM ** -0.5
    kc = compress_tokens(k_cmp, pe_k, w1_k, w2_k)
    vc = compress_tokens(v_cmp, pe_v, w1_v, w2_v)
    n_cmp = kc.shape[1]
    cmp_start = np.arange(n_cmp) * CMP_STRIDE
    cmp_end = jnp.asarray(cmp_start + CMP_LEN - 1)
    n_slc = L // SLC_LEN
    n_sel = min(SLC_TOPN, n_slc)
    slc_start = np.arange(n_slc) * SLC_LEN
    overlap = (cmp_start[:, None] < slc_start[None, :] + SLC_LEN) & (cmp_start[:, None] + CMP_LEN > slc_start[None, :])
    cmp_to_slc = jnp.asarray(overlap.astype(np.float32))
    k_blk = k_slc.reshape(bsz, n_slc, SLC_LEN, B_KV_HEADS, HEAD_DIM).transpose(0, 3, 1, 2, 4)
    v_blk = v_slc.reshape(bsz, n_slc, SLC_LEN, B_KV_HEADS, HEAD_DIM).transpose(0, 3, 1, 2, 4)
    k_pad = jnp.pad(k_win, ((0, 0), (WINDOW, 0), (0, 0), (0, 0)))
    v_pad = jnp.pad(v_win, ((0, 0), (WINDOW, 0), (0, 0), (0, 0)))
    b_ix = jnp.arange(bsz)[:, None, None, None]
    h_ix = jnp.arange(B_KV_HEADS)[None, None, :, None]
    j_ix = jnp.arange(n_slc)

    def block(i):
        t0 = i * Q_BLOCK
        qpos = t0 + jnp.arange(Q_BLOCK)
        qn = _rows(q_nope, t0).reshape(bsz, Q_BLOCK, B_KV_HEADS, B_GROUP, HEAD_DIM)
        qr = _rows(q_rope, t0).reshape(bsz, Q_BLOCK, B_KV_HEADS, B_GROUP, HEAD_DIM)
        g = _rows(gates, t0)
        s = jnp.einsum('btkgd,bnkd->btkgn', qn, kc) * scale
        m_cmp = (cmp_end[None, :] <= qpos[:, None])[None, :, None, None, :]
        p_cmp = masked_softmax(s, m_cmp)
        o_cmp = jnp.einsum('btkgn,bnkd->btkgd', p_cmp.astype(vc.dtype), vc)
        imp = jnp.einsum('btkgn,nj->btkj', p_cmp, cmp_to_slc)
        admiss = j_ix[None, :] * SLC_LEN <= qpos[:, None]
        cur = qpos[:, None] // SLC_LEN
        forced = admiss & ((j_ix[None, :] == 0) | (j_ix[None, :] == cur) | (j_ix[None, :] == cur - 1))
        score = jnp.where(admiss[None, :, None, :], imp, -jnp.inf)
        score = jnp.where(forced[None, :, None, :], jnp.inf, score)
        _, sel = lax.top_k(score, n_sel)
        ks = k_blk[b_ix, h_ix, sel].reshape(bsz, Q_BLOCK, B_KV_HEADS, n_sel * SLC_LEN, HEAD_DIM)
        vs = v_blk[b_ix, h_ix, sel].reshape(bsz, Q_BLOCK, B_KV_HEADS, n_sel * SLC_LEN, HEAD_DIM)
        tok = (sel[..., None] * SLC_LEN + jnp.arange(SLC_LEN)).reshape(bsz, Q_BLOCK, B_KV_HEADS, n_sel * SLC_LEN)
        m_sel = (tok <= qpos[None, :, None, None])[:, :, :, None, :]
        s = jnp.einsum('btkgd,btksd->btkgs', qr, ks) * scale
        p = masked_softmax(s, m_sel).astype(vs.dtype)
        o_slc = jnp.einsum('btkgs,btksd->btkgd', p, vs)
        kw = lax.dynamic_slice_in_dim(k_pad, t0, WINDOW + Q_BLOCK, axis=1)
        vw = lax.dynamic_slice_in_dim(v_pad, t0, WINDOW + Q_BLOCK, axis=1)
        kpos = t0 - WINDOW + jnp.arange(WINDOW + Q_BLOCK)
        rel = qpos[:, None] - kpos[None, :]
        m_win = ((rel >= 0) & (rel < WINDOW) & (kpos[None, :] >= 0))[None, :, None, None, :]
        s = jnp.einsum('btkgd,bskd->btkgs', qr, kw) * scale
        p = masked_softmax(s, m_win).astype(vw.dtype)
        o_win = jnp.einsum('btkgs,bskd->btkgd', p, vw)
        gc = g[:, :, 0].reshape(bsz, Q_BLOCK, B_KV_HEADS, B_GROUP, 1)
        gs = g[:, :, 1].reshape(bsz, Q_BLOCK, B_KV_HEADS, B_GROUP, 1)
        gw = g[:, :, 2].reshape(bsz, Q_BLOCK, B_KV_HEADS, B_GROUP, 1)
        o = gc * o_cmp + gs * o_slc + gw * o_win
        return o.reshape(bsz, Q_BLOCK, B_HEADS * HEAD_DIM)

    out = lax.map(block, jnp.arange(L // Q_BLOCK))
    return out.transpose(1, 0, 2, 3).reshape(bsz, L, B_HEADS * HEAD_DIM)


def mixer_layer(h, positions, w_in, pe_k, w1_k, w2_k, pe_v, w1_v, w2_v, w_br_a, w_br_b, w_out):
    bsz, L, _ = h.shape
    proj = jnp.einsum('bsd,dc->bsc', h, w_in)
    qa, ka, va, qi, ki, wi, qb, kvb, gb, gate_a, gate_b = jnp.split(proj, SPLIT_POINTS, axis=-1)
    qa = rope(qa.reshape(bsz, L, A_HEADS, HEAD_DIM), positions)
    ka = rope(ka[:, :, None, :], positions)[:, :, 0]
    qi = rope(qi.reshape(bsz, L, IDX_HEADS, IDX_DIM), positions)
    ki = rope(ki[:, :, None, :], positions)[:, :, 0]
    o_a = dsa_attention(qa, ka, va, qi, ki, wi)
    qb = qb.reshape(bsz, L, B_HEADS, HEAD_DIM)
    kvb = kvb.reshape(bsz, L, 6, B_KV_HEADS, HEAD_DIM)
    k_cmp, v_cmp = kvb[:, :, 0], kvb[:, :, 1]
    k_slc, v_slc = rope(kvb[:, :, 2], positions), kvb[:, :, 3]
    k_win, v_win = rope(kvb[:, :, 4], positions), kvb[:, :, 5]
    nsa_gates = jax.nn.sigmoid(gb.reshape(bsz, L, 3, B_HEADS))
    o_b = nsa_attention(qb, rope(qb, positions), k_cmp, v_cmp, k_slc, v_slc, k_win, v_win, nsa_gates,
                        pe_k, w1_k, w2_k, pe_v, w1_v, w2_v)
    merged = jax.nn.sigmoid(gate_a) * (o_a @ w_br_a) + jax.nn.sigmoid(gate_b) * (o_b @ w_br_b)
    return merged @ w_out


def hier_moe(h, w_group, b_group, w_expert, b_expert, w_gate_up, w_down):
    bsz, L, d = h.shape
    n_tok = bsz * L
    tok = h.reshape(n_tok, d)
    g_logit = jnp.einsum('nd,dg->ng', tok, w_group).astype(jnp.float32) + b_group.astype(jnp.float32)
    g_prob = jax.nn.softmax(g_logit, axis=-1)
    g_sel = jnp.argmax(g_logit, axis=-1)
    e_logit = jnp.einsum('nd,de->ne', tok, w_expert).astype(jnp.float32) + b_expert.astype(jnp.float32)
    e_in = jnp.take_along_axis(e_logit.reshape(n_tok, N_GROUPS, EXPERTS_PER_GROUP), g_sel[:, None, None], axis=1)[:, 0]
    top_val, top_idx = lax.top_k(e_in, TOP_K_INNER)
    weight = jnp.take_along_axis(g_prob, g_sel[:, None], axis=1) * jax.nn.softmax(top_val, axis=-1)
    expert = g_sel[:, None] * EXPERTS_PER_GROUP + top_idx
    n_asg = n_tok * TOP_K_INNER
    e_flat = expert.reshape(n_asg)
    tok_flat = jnp.repeat(jnp.arange(n_tok), TOP_K_INNER)
    w_flat = weight.reshape(n_asg)
    order = jnp.argsort(e_flat)
    e_s, tok_s, w_s = e_flat[order], tok_flat[order], w_flat[order]
    counts = jnp.bincount(e_flat, length=N_EXPERTS)
    starts = jnp.cumsum(counts) - counts
    padded = (counts + MOE_BLOCK - 1) // MOE_BLOCK * MOE_BLOCK
    pad_end = jnp.cumsum(padded)
    pad_start = pad_end - padded
    dest = pad_start[e_s] + jnp.arange(n_asg) - starts[e_s]
    n_blocks = -(-n_asg // MOE_BLOCK) + N_EXPERTS
    buf = jnp.zeros((n_blocks * MOE_BLOCK, d), h.dtype).at[dest].set(tok[tok_s])
    blk_expert = jnp.minimum(jnp.searchsorted(pad_end, jnp.arange(n_blocks) * MOE_BLOCK, side='right'), N_EXPERTS - 1)

    def run(args):
        xb, e = args
        gate, up = jnp.split(xb @ w_gate_up[e], 2, axis=-1)
        return (jax.nn.silu(gate) * up) @ w_down[e]

    y_buf = lax.map(run, (buf.reshape(n_blocks, MOE_BLOCK, d), blk_expert)).reshape(n_blocks * MOE_BLOCK, d)
    y = jax.ops.segment_sum(y_buf[dest] * w_s[:, None].astype(h.dtype), tok_s, num_segments=n_tok)
    return y.reshape(bsz, L, d)


def setup_inputs(seed: int = 0) -> dict:
    key = jax.random.key(seed)
    ks = jax.random.split(key, 24)

    def nrm(k, shape, scale):
        return jax.random.normal(k, shape, jnp.float32) * scale

    cmp_in = CMP_LEN * HEAD_DIM
    a_w = A_HEADS * HEAD_DIM
    b_w = B_HEADS * HEAD_DIM
    offs = jax.random.randint(ks[1], (BATCH, 1), 0, 1024, dtype=jnp.int32)
    return {
        "x": nrm(ks[0], (BATCH, SEQ, D_MODEL), 1.0),
        "positions": offs + jnp.arange(SEQ, dtype=jnp.int32)[None, :],
        "norm_mix": 1.0 + nrm(ks[2], (DEPTH, D_MODEL), 0.02),
        "w_in": nrm(ks[3], (DEPTH, D_MODEL, IN_WIDTH), D_MODEL ** -0.5),
        "pe_k": nrm(ks[4], (DEPTH, CMP_LEN, HEAD_DIM), 0.1),
        "w1_k": nrm(ks[5], (DEPTH, cmp_in, CMP_HIDDEN), cmp_in ** -0.5),
        "w2_k": nrm(ks[6], (DEPTH, CMP_HIDDEN, HEAD_DIM), CMP_HIDDEN ** -0.5),
        "pe_v": nrm(ks[7], (DEPTH, CMP_LEN, HEAD_DIM), 0.1),
        "w1_v": nrm(ks[8], (DEPTH, cmp_in, CMP_HIDDEN), cmp_in ** -0.5),
        "w2_v": nrm(ks[9], (DEPTH, CMP_HIDDEN, HEAD_DIM), CMP_HIDDEN ** -0.5),
        "w_br_a": nrm(ks[10], (DEPTH, a_w, D_MODEL), a_w ** -0.5),
        "w_br_b": nrm(ks[11], (DEPTH, b_w, D_MODEL), b_w ** -0.5),
        "w_out": nrm(ks[12], (DEPTH, D_MODEL, D_MODEL), D_MODEL ** -0.5),
        "norm_ffn": 1.0 + nrm(ks[13], (DEPTH, D_MODEL), 0.02),
        "w_group": nrm(ks[14], (DEPTH, D_MODEL, N_GROUPS), D_MODEL ** -0.5),
        "b_group": nrm(ks[15], (DEPTH, N_GROUPS), 0.01),
        "w_expert": nrm(ks[16], (DEPTH, D_MODEL, N_EXPERTS), D_MODEL ** -0.5),
        "b_expert": nrm(ks[17], (DEPTH, N_EXPERTS), 0.01),
        "w_gate_up": nrm(ks[18], (DEPTH, N_EXPERTS, D_MODEL, 2 * D_EXPERT), D_MODEL ** -0.5),
        "w_down": nrm(ks[19], (DEPTH, N_EXPERTS, D_EXPERT, D_MODEL), D_EXPERT ** -0.5),
        "norm_final": 1.0 + nrm(ks[20], (D_MODEL,), 0.02),
    }


def reference(x, positions, norm_mix, w_in, pe_k, w1_k, w2_k, pe_v, w1_v, w2_v, w_br_a, w_br_b,
              w_out, norm_ffn, w_group, b_group, w_expert, b_expert, w_gate_up, w_down, norm_final):
    for l in range(DEPTH):
        h = rmsnorm(x, norm_mix[l])
        x = x + mixer_layer(h, positions, w_in[l], pe_k[l], w1_k[l], w2_k[l], pe_v[l], w1_v[l], w2_v[l],
                            w_br_a[l], w_br_b[l], w_out[l])
        h = rmsnorm(x, norm_ffn[l])
        x = x + hier_moe(h, w_group[l], b_group[l], w_expert[l], b_expert[l], w_gate_up[l], w_down[l])
    return rmsnorm(x, norm_final)
```

```python
import functools

import numpy as np
import jax
import jax.numpy as jnp
from jax import lax
from jax.experimental import pallas as pl
from jax.experimental.pallas import tpu as pltpu

D_MODEL = 1024
HEAD_DIM = 64
ROPE_THETA = 10000.0
RMS_EPS = 1e-6
A_HEADS = 8
IDX_HEADS = 8
IDX_DIM = 32
DSA_TOPK_MAX = 256
B_HEADS = 8
B_KV_HEADS = 2
B_GROUP = B_HEADS // B_KV_HEADS
CMP_LEN = 32
CMP_STRIDE = 16
CMP_HIDDEN = 128
SLC_LEN = 64
SLC_TOPN = 16
WINDOW = 512
N_GROUPS = 4
EXPERTS_PER_GROUP = 8
N_EXPERTS = N_GROUPS * EXPERTS_PER_GROUP
D_EXPERT = 256
MOE_BLOCK = 256

IN_SPLITS = (A_HEADS * HEAD_DIM, HEAD_DIM, HEAD_DIM, IDX_HEADS * IDX_DIM, IDX_DIM, IDX_HEADS,
             B_HEADS * HEAD_DIM, 6 * B_KV_HEADS * HEAD_DIM, 3 * B_HEADS, D_MODEL, D_MODEL)
IN_OFFS = tuple(int(v) for v in np.cumsum((0,) + IN_SPLITS)[:-1])

LANES = 128
MASK_NEG = -3e30
M_INIT = -1e30
VMEM_LIMIT = 48 * 1024 * 1024
SCALE = HEAD_DIM ** -0.5

F32 = jnp.float32
BF16 = jnp.bfloat16
I32 = jnp.int32


def _cparams(sem):
    return pltpu.CompilerParams(dimension_semantics=sem, vmem_limit_bytes=VMEM_LIMIT)


P_QA = 0
P_KVA = 512
P_QI = 640
P_KIW = 896
P_QB = 1024
P_KVB = 1536
P_CMP = 2048
P_GB = 2304
P_GATE = 2432
P_WIDTH = 4480


def _in_perm():
    idx = -np.ones((P_WIDTH,), np.int64)
    o_qa, o_ka, o_va, o_qi, o_ki, o_wi, o_qb, o_kvb, o_gb, o_ga, o_gbb = IN_OFFS
    idx[P_QA:P_QA + 512] = o_qa + np.arange(512)
    idx[P_KVA:P_KVA + 64] = o_ka + np.arange(64)
    idx[P_KVA + 64:P_KVA + 128] = o_va + np.arange(64)
    idx[P_QI:P_QI + 256] = o_qi + np.arange(256)
    idx[P_KIW:P_KIW + 32] = o_ki + np.arange(32)
    idx[P_KIW + 32:P_KIW + 40] = o_wi + np.arange(8)
    idx[P_QB:P_QB + 512] = o_qb + np.arange(512)

    def kvb_col(s, k):
        return o_kvb + s * (B_KV_HEADS * HEAD_DIM) + k * HEAD_DIM + np.arange(HEAD_DIM)

    for k in range(B_KV_HEADS):
        for j, s in enumerate((2, 3, 4, 5)):
            lo = P_KVB + k * 256 + j * 64
            idx[lo:lo + 64] = kvb_col(s, k)
    for j, (s, k) in enumerate(((0, 0), (0, 1), (1, 0), (1, 1))):
        lo = P_CMP + j * 64
        idx[lo:lo + 64] = kvb_col(s, k)
    idx[P_GB:P_GB + 24] = o_gb + np.arange(24)
    idx[P_GATE:P_GATE + 1024] = o_ga + np.arange(1024)
    idx[P_GATE + 1024:P_GATE + 2048] = o_gbb + np.arange(1024)
    return idx


_IN_PERM = _in_perm()


def _permute_columns(w):
    pieces, start = [], 0
    for pos in range(1, P_WIDTH + 1):
        if pos < P_WIDTH:
            prev, cur = int(_IN_PERM[pos - 1]), int(_IN_PERM[pos])
            same_run = (prev < 0 and cur < 0) or (prev >= 0 and cur == prev + 1)
        if pos == P_WIDTH or not same_run:
            src = int(_IN_PERM[start])
            pieces.append(jnp.zeros((w.shape[0], pos - start), w.dtype) if src < 0 else w[:, src:src + pos - start])
            start = pos
    return jnp.concatenate(pieces, axis=1)


def _trig_kernel(pos_ref, freq_ref, cos_ref, sin_ref):
    ang = pos_ref[...] * freq_ref[0]
    cos_ref[...] = jnp.cos(ang)
    sin_ref[...] = jnp.sin(ang)


def _rope_tables(positions):
    n = positions.size
    pos = positions.reshape(n).astype(F32)
    h64, h32 = HEAD_DIM // 2, IDX_DIM // 2
    f64 = ROPE_THETA ** (-jnp.arange(h64, dtype=F32) / h64)
    f32_ = ROPE_THETA ** (-jnp.arange(h32, dtype=F32) / h32)
    rows64, rows32 = n * h64 // LANES, n * h32 // LANES
    pos_rep = jnp.concatenate([jnp.repeat(pos, h64).reshape(rows64, LANES),
                               jnp.repeat(pos, h32).reshape(rows32, LANES)], axis=0)
    freq = jnp.stack([jnp.tile(f64, LANES // h64), jnp.tile(f32_, LANES // h32)])[:, None, :]
    rows = rows64 + rows32
    tr = rows32 if rows32 <= 1024 else 1024
    assert rows64 % tr == 0 and rows32 % tr == 0
    n64_tiles = rows64 // tr
    cos, sin = pl.pallas_call(
        _trig_kernel,
        grid=(rows // tr,),
        in_specs=[pl.BlockSpec((tr, LANES), lambda i: (i, 0)),
                  pl.BlockSpec((1, 1, LANES), lambda i: (jnp.where(i >= n64_tiles, 1, 0), 0, 0))],
        out_specs=[pl.BlockSpec((tr, LANES), lambda i: (i, 0))] * 2,
        out_shape=[jax.ShapeDtypeStruct((rows, LANES), F32)] * 2,
        compiler_params=_cparams(("parallel",)),
        name="trig",
    )(pos_rep, freq)
    c64 = cos[:rows64].reshape(n, h64)
    s64 = sin[:rows64].reshape(n, h64)
    c32 = cos[rows64:].reshape(n, h32)
    s32 = sin[rows64:].reshape(n, h32)
    cos64 = jnp.tile(c64, (1, LANES // h64))
    sin64 = jnp.tile(jnp.concatenate([-s64, s64], axis=1), (1, LANES // HEAD_DIM))
    cos32 = jnp.tile(c32, (1, LANES // h32))
    sin32 = jnp.tile(jnp.concatenate([-s32, s32], axis=1), (1, LANES // IDX_DIM))
    return cos64, sin64, cos32, sin32


def _rope_slab(x, cos, sin, half):
    lane = lax.broadcasted_iota(I32, x.shape, 1)
    first = (lane % (2 * half)) < half
    rot = jnp.where(first, pltpu.roll(x, LANES - half, 1), pltpu.roll(x, half, 1))
    return x * cos + rot * sin


def _inproj_kernel(x_ref, g_ref, w_ref, c64_ref, s64_ref, c32_ref, s32_ref,
                   qa_ref, kva_ref, qi_ref, kiw_ref, qbn_ref, qbr_ref, kvb_ref, cmp_ref, gsig_ref, gate_ref):
    x = x_ref[...]
    ms = jnp.mean(x * x, axis=-1, keepdims=True)
    h = (x * lax.rsqrt(ms + RMS_EPS) * g_ref[...]).astype(BF16)
    c64, s64, c32, s32 = c64_ref[...], s64_ref[...], c32_ref[...], s32_ref[...]
    lane = lax.broadcasted_iota(I32, c64.shape, 1)

    def proj(lo, width):
        return jnp.dot(h, w_ref[:, lo:lo + width], preferred_element_type=F32)

    def slab(p, j):
        return p[:, j * LANES:(j + 1) * LANES]

    p = proj(P_QA, 512)
    for j in range(4):
        qa_ref[:, j * LANES:(j + 1) * LANES] = (_rope_slab(slab(p, j), c64, s64, 32) * SCALE).astype(BF16)
    p = proj(P_KVA, 128)
    kva_ref[...] = jnp.where(lane < HEAD_DIM, _rope_slab(p, c64, s64, 32), p).astype(BF16)
    p = proj(P_QI, 256)
    for j in range(2):
        qi_ref[:, j * LANES:(j + 1) * LANES] = _rope_slab(slab(p, j), c32, s32, 16).astype(BF16)
    p = proj(P_KIW, 128)
    kiw_ref[...] = jnp.where(lane < IDX_DIM, _rope_slab(p, c32, s32, 16), p)
    p = proj(P_QB, 512)
    qbn_ref[...] = (p * SCALE).astype(BF16)
    for j in range(4):
        qbr_ref[:, j * LANES:(j + 1) * LANES] = (_rope_slab(slab(p, j), c64, s64, 32) * SCALE).astype(BF16)
    p = proj(P_KVB, 512)
    for j in range(4):
        pj = slab(p, j)
        kvb_ref[:, j * LANES:(j + 1) * LANES] = jnp.where(lane < HEAD_DIM, _rope_slab(pj, c64, s64, 32), pj).astype(BF16)
    cmp_ref[...] = proj(P_CMP, 256)
    gsig_ref[...] = jax.nn.sigmoid(proj(P_GB, 128))
    for j in range(2):
        gate_ref[:, j * 1024:(j + 1) * 1024] = jax.nn.sigmoid(proj(P_GATE + j * 1024, 1024)).astype(BF16)


def _inproj(x2, g, w_p, tables, tm):
    n = x2.shape[0]
    row = lambda w: pl.BlockSpec((tm, w), lambda i: (i, 0))
    full = lambda a: pl.BlockSpec(a.shape, lambda i: (0, 0))
    outs = [(512, BF16), (128, BF16), (256, BF16), (128, F32), (512, BF16), (512, BF16), (512, BF16),
            (256, F32), (128, F32), (2048, BF16)]
    return pl.pallas_call(
        _inproj_kernel,
        grid=(n // tm,),
        in_specs=[row(D_MODEL), full(g), full(w_p)] + [row(LANES)] * 4,
        out_specs=[row(w) for w, _ in outs],
        out_shape=[jax.ShapeDtypeStruct((n, w), dt) for w, dt in outs],
        compiler_params=_cparams(("parallel",)),
        name="inproj",
    )(x2, g, w_p, *tables)


def _compress_kernel(g_ref, pe_ref, w1_ref, w2_ref, o_ref):
    g = g_ref[0, 0]
    half = CMP_STRIDE * HEAD_DIM
    a = jnp.dot((g + pe_ref[0, :, :half]).astype(BF16), w1_ref[0, :half, :], preferred_element_type=F32)
    b = jnp.dot((g + pe_ref[0, :, half:]).astype(BF16), w1_ref[0, half:, :], preferred_element_type=F32)
    n_grp = g.shape[0]
    hid = a + pltpu.roll(b, n_grp - 1, 0)
    hid = hid * jax.nn.sigmoid(hid)
    o_ref[0, 0] = jnp.dot(hid.astype(BF16), w2_ref[0], preferred_element_type=F32).astype(BF16)


def _compress(cmp4, pe, w1, w2):
    bsz, _, n_grp, width = cmp4.shape
    return pl.pallas_call(
        _compress_kernel,
        grid=(bsz, 4),
        in_specs=[pl.BlockSpec((1, 1, n_grp, width), lambda b, a: (b, a, 0, 0)),
                  pl.BlockSpec((1, 1, 2 * width), lambda b, a: (a // 2, 0, 0)),
                  pl.BlockSpec((1, 2 * width, CMP_HIDDEN), lambda b, a: (a // 2, 0, 0)),
                  pl.BlockSpec((1, CMP_HIDDEN, HEAD_DIM), lambda b, a: (a // 2, 0, 0))],
        out_specs=pl.BlockSpec((1, 1, n_grp, HEAD_DIM), lambda b, a: (b, a, 0, 0)),
        out_shape=jax.ShapeDtypeStruct((bsz, 4, n_grp, HEAD_DIM), BF16),
        compiler_params=_cparams(("parallel", "parallel")),
        name="compress",
    )(cmp4, pe, w1, w2)


def _nt_dot(a, b):
    return lax.dot_general(a, b, (((1,), (1,)), ((), ())), preferred_element_type=F32)


def _stack_heads(q, n_heads):
    t = q.shape[0]
    lane = lax.broadcasted_iota(I32, (t, LANES), 1)
    parts = []
    for h in range(n_heads):
        slab = q[:, (h // 2) * LANES:(h // 2 + 1) * LANES].astype(F32)
        if h % 2:
            slab = pltpu.roll(slab, HEAD_DIM, 1)
        parts.append(jnp.where(lane < HEAD_DIM, slab, 0.0).astype(BF16))
    return jnp.concatenate(parts, axis=0)


def _unstack_heads(o, n_heads, t):
    lane = lax.broadcasted_iota(I32, (t, LANES), 1)
    slabs = []
    for j in range(n_heads // 2):
        even = pltpu.roll(o[(2 * j) * t:(2 * j + 1) * t], HEAD_DIM, 1)
        slabs.append(jnp.where(lane < HEAD_DIM, even, o[(2 * j + 1) * t:(2 * j + 2) * t]))
    return jnp.concatenate(slabs, axis=1)


def _online_update(s, kv, m_ref, l_ref, acc_ref):
    m_old = m_ref[...]
    m_new = jnp.maximum(m_old, jnp.max(s, axis=-1, keepdims=True))
    alpha = jnp.exp(m_old - m_new)
    p = jnp.exp(s - m_new)
    l_ref[...] = alpha * l_ref[...] + jnp.sum(p, axis=-1, keepdims=True)
    acc_ref[...] = alpha * acc_ref[...] + jnp.dot(p.astype(BF16), kv, preferred_element_type=F32)
    m_ref[...] = m_new


def _reset(m_ref, l_ref, acc_ref):
    m_ref[...] = jnp.full(m_ref.shape, M_INIT, F32)
    l_ref[...] = jnp.zeros(l_ref.shape, F32)
    acc_ref[...] = jnp.zeros(acc_ref.shape, F32)


INT_MIN = -2 ** 31


def _dsa_kernel(qa_ref, qi_ref, wq_ref, kva_ref, kiw_ref, o_ref,
                key_scr, m_scr, l_scr, acc_scr, *, t, ck, k_sel):
    i = pl.program_id(1)
    t0 = i * t
    n_chunks = (t0 + t + ck - 1) // ck
    qpos = t0 + lax.broadcasted_iota(I32, (t, 1), 0)
    lane_k = lax.broadcasted_iota(I32, (t, ck), 1)

    qi = qi_ref[0]
    wq = wq_ref[0]

    def score_chunk(c, carry):
        k0 = pl.multiple_of(c * ck, ck)
        kic = kiw_ref[0, pl.ds(k0, ck), :][:, :IDX_DIM].astype(BF16)
        acc = jnp.zeros((t, ck), F32)
        for h in range(IDX_HEADS):
            lg = _nt_dot(qi[:, h * IDX_DIM:(h + 1) * IDX_DIM], kic)
            acc = acc + wq[:, IDX_DIM + h:IDX_DIM + h + 1] * jnp.maximum(lg, 0.0)
        acc = jnp.where(k0 + lane_k <= qpos, acc, -jnp.inf)
        bits = pltpu.bitcast(acc, I32)
        key_scr[:, pl.ds(k0, ck)] = jnp.where(bits < 0, bits ^ 0x7FFFFFFF, bits)
        return carry

    lax.fori_loop(0, n_chunks, score_chunk, 0)

    def count_ge(cand):
        def body(c, cnt):
            k0 = pl.multiple_of(c * ck, ck)
            ge = key_scr[:, pl.ds(k0, ck)] >= cand
            return cnt + jnp.sum(jnp.where(ge, 1.0, 0.0), axis=-1, keepdims=True)
        return lax.fori_loop(0, n_chunks, body, jnp.zeros((t, 1), F32))

    def bit_step(b, lo):
        cand = lo + lax.shift_left(jnp.int32(1), 31 - b)
        return jnp.where(count_ge(cand) >= k_sel, cand, lo)

    thr = lax.fori_loop(0, 32, bit_step, jnp.full((t, 1), INT_MIN, I32))
    n_gt = count_ge(thr + 1)
    n_ge = count_ge(thr)
    need = k_sel - n_gt

    def tie_bound():
        def count_tie_below(j):
            def body(c, cnt):
                k0 = pl.multiple_of(c * ck, ck)
                hit = (key_scr[:, pl.ds(k0, ck)] == thr) & (k0 + lane_k < j)
                return cnt + jnp.sum(jnp.where(hit, 1.0, 0.0), axis=-1, keepdims=True)
            return lax.fori_loop(0, n_chunks, body, jnp.zeros((t, 1), F32))

        n_bits = int(key_scr.shape[1]).bit_length()

        def step(b, j):
            cand = j + lax.shift_left(jnp.int32(1), n_bits - 1 - b)
            return jnp.where(count_tie_below(cand) <= need, cand, j)
        return lax.fori_loop(0, n_bits, step, jnp.zeros((t, 1), I32))

    has_excess = jnp.max(jnp.where(n_ge > k_sel, 1.0, 0.0)) > 0.0
    jbound = lax.cond(has_excess, tie_bound, lambda: jnp.full((t, 1), 2 ** 30, I32))

    q = _stack_heads(qa_ref[0], A_HEADS)
    _reset(m_scr, l_scr, acc_scr)

    def attn_chunk(c, carry):
        k0 = pl.multiple_of(c * ck, ck)
        key = key_scr[:, pl.ds(k0, ck)]
        idx = k0 + lane_k
        sel = ((key > thr) | ((key == thr) & (idx < jbound))) & (idx <= qpos)
        bias = jnp.where(sel, 0.0, MASK_NEG)
        kv = kva_ref[0, pl.ds(k0, ck), :]
        s = _nt_dot(q, kv)
        s = (s.reshape(A_HEADS, t, ck) + bias[None]).reshape(A_HEADS * t, ck)
        _online_update(s, kv, m_scr, l_scr, acc_scr)
        return carry

    lax.fori_loop(0, n_chunks, attn_chunk, 0)
    o = acc_scr[...] / l_scr[...]
    o_ref[0] = _unstack_heads(o, A_HEADS, t).astype(BF16)


def _dsa(qa, qi, kiw, kva, t, ck):
    bsz, seq, _ = qa.shape
    k_sel = min(DSA_TOPK_MAX, seq // 4)
    kern = functools.partial(_dsa_kernel, t=t, ck=ck, k_sel=k_sel)
    blk = lambda w: pl.BlockSpec((1, t, w), lambda b, i: (b, i, 0))
    res = lambda w: pl.BlockSpec((1, seq, w), lambda b, i: (b, 0, 0))
    rows = A_HEADS * t
    return pl.pallas_call(
        kern,
        grid=(bsz, seq // t),
        in_specs=[blk(512), blk(256), blk(LANES), res(LANES), res(LANES)],
        out_specs=blk(512),
        out_shape=jax.ShapeDtypeStruct((bsz, seq, 512), BF16),
        scratch_shapes=[pltpu.VMEM((t, seq), I32), pltpu.VMEM((rows, 1), F32), pltpu.VMEM((rows, 1), F32),
                        pltpu.VMEM((rows, LANES), F32)],
        compiler_params=_cparams(("parallel", "arbitrary")),
        name="dsa",
    )(qa, qi, kiw, kva, kiw)


def _nsa_kernel(qn_ref, qr_ref, g_ref, kcv_ref, kv_ref, c2s_ref, oh_ref, o_ref,
                m_scr, l_scr, acc_scr, m2_scr, l2_scr, acc2_scr, *, t, ck, cw, n_slc, n_sel):
    kh = pl.program_id(1)
    i = pl.program_id(2)
    t0 = i * t
    rows = B_GROUP * t
    qpos = t0 + lax.broadcasted_iota(I32, (t, 1), 0)
    qn = _stack_heads(qn_ref[0], B_GROUP)
    qr = _stack_heads(qr_ref[0], B_GROUP)

    kcv = kcv_ref[0, 0]
    n_cp = kcv.shape[0]
    s = _nt_dot(qn, kcv).reshape(B_GROUP, t, n_cp)
    cmp_end = lax.broadcasted_iota(I32, (t, n_cp), 1) * CMP_STRIDE + (CMP_LEN - 1)
    vis = (cmp_end <= qpos)[None]
    s = jnp.where(vis, s, M_INIT)
    p = jnp.where(vis, jnp.exp(s - jnp.max(s, axis=-1, keepdims=True)), 0.0)
    den = jnp.sum(p, axis=-1, keepdims=True)
    p = p * jnp.where(den > 0.0, 1.0 / den, 0.0)
    o_cmp = jnp.dot(p.reshape(rows, n_cp).astype(BF16), kcv, preferred_element_type=F32)

    psum = p[0]
    for g in range(1, B_GROUP):
        psum = psum + p[g]
    p_hi = psum.astype(BF16)
    p_lo = (psum - p_hi.astype(F32)).astype(BF16)
    imp = (jnp.dot(p_hi, c2s_ref[...], preferred_element_type=F32)
           + jnp.dot(p_lo, c2s_ref[...], preferred_element_type=F32))
    jx = lax.broadcasted_iota(I32, (t, n_slc), 1)
    admiss = jx * SLC_LEN <= qpos
    cur = lax.shift_right_logical(qpos, SLC_LEN.bit_length() - 1)
    forced = admiss & ((jx == 0) | (jx == cur) | (jx == cur - 1))
    score = jnp.where(forced, jnp.inf, jnp.where(admiss, imp, -jnp.inf))
    rank = jnp.zeros((t, n_slc), F32)
    for j2 in range(n_slc):
        col = score[:, j2:j2 + 1]
        ahead = (col > score) | ((col == score) & (j2 < jx))
        rank = rank + jnp.where(ahead, 1.0, 0.0)
    selbias = jnp.where(rank < n_sel, 0.0, MASK_NEG).astype(BF16)

    _reset(m_scr, l_scr, acc_scr)
    lane_k = lax.broadcasted_iota(I32, (t, ck), 1)
    n_chunks = (t0 + t + ck - 1) // ck

    def slc_chunk(c, carry):
        k0 = pl.multiple_of(c * ck, ck)
        kv = kv_ref[0, pl.ds(k0, ck), :LANES]
        bias = _nt_dot(selbias, oh_ref[pl.ds(k0, ck), :])
        bias = jnp.where(k0 + lane_k <= qpos, bias, MASK_NEG)
        sc = (_nt_dot(qr, kv).reshape(B_GROUP, t, ck) + bias[None]).reshape(rows, ck)
        _online_update(sc, kv, m_scr, l_scr, acc_scr)
        return carry

    lax.fori_loop(0, n_chunks, slc_chunk, 0)
    o_slc = acc_scr[...] / l_scr[...]

    _reset(m2_scr, l2_scr, acc2_scr)
    lane_w = lax.broadcasted_iota(I32, (t, cw), 1)
    for c in range((WINDOW + t) // cw):
        w0 = t0 - WINDOW + c * cw

        @pl.when(w0 >= 0)
        def _():
            k0 = pl.multiple_of(w0, cw)
            kv = kv_ref[0, pl.ds(k0, cw), LANES:]
            rel = qpos - (k0 + lane_w)
            bias = jnp.where((rel >= 0) & (rel < WINDOW), 0.0, MASK_NEG)
            sc = (_nt_dot(qr, kv).reshape(B_GROUP, t, cw) + bias[None]).reshape(rows, cw)
            _online_update(sc, kv, m2_scr, l2_scr, acc2_scr)

    o_win = acc2_scr[...] / l2_scr[...]

    gates = g_ref[0]
    glane = lax.broadcasted_iota(I32, gates.shape, 1)
    outs = []
    for g in range(B_GROUP):
        hd = kh * B_GROUP + g

        def gate(branch):
            return jnp.sum(jnp.where(glane == branch * B_HEADS + hd, gates, 0.0), axis=-1, keepdims=True)

        sl = slice(g * t, (g + 1) * t)
        outs.append(gate(0) * o_cmp[sl] + gate(1) * o_slc[sl] + gate(2) * o_win[sl])
    o_ref[0] = _unstack_heads(jnp.concatenate(outs, axis=0), B_GROUP, t).astype(BF16)


def _nsa(qbn, qbr, gsig, kcv, kvb, t, ck, cw):
    bsz, seq, _ = qbn.shape
    n_cp = kcv.shape[2]
    n_slc = seq // SLC_LEN
    n_sel = min(SLC_TOPN, n_slc)
    n = np.arange(n_cp)
    js = np.arange(n_slc) * SLC_LEN
    overlap = (n[:, None] * CMP_STRIDE < js[None, :] + SLC_LEN) & (n[:, None] * CMP_STRIDE + CMP_LEN > js[None, :])
    overlap &= (n[:, None] < n_cp - 1)
    c2s = jnp.asarray(overlap.astype(np.float32), BF16)
    onehot = jnp.asarray((np.arange(seq)[:, None] // SLC_LEN == np.arange(n_slc)[None, :]).astype(np.float32), BF16)
    kern = functools.partial(_nsa_kernel, t=t, ck=ck, cw=cw, n_slc=n_slc, n_sel=n_sel)
    rows = B_GROUP * t
    width = B_GROUP * HEAD_DIM
    qblk = pl.BlockSpec((1, t, width), lambda b, k, i: (b, i, k))
    return pl.pallas_call(
        kern,
        grid=(bsz, B_KV_HEADS, seq // t),
        in_specs=[qblk, qblk,
                  pl.BlockSpec((1, t, LANES), lambda b, k, i: (b, i, 0)),
                  pl.BlockSpec((1, 1, n_cp, LANES), lambda b, k, i: (b, k, 0, 0)),
                  pl.BlockSpec((1, seq, 4 * HEAD_DIM), lambda b, k, i: (b, 0, k)),
                  pl.BlockSpec((n_cp, n_slc), lambda b, k, i: (0, 0)),
                  pl.BlockSpec((seq, n_slc), lambda b, k, i: (0, 0))],
        out_specs=qblk,
        out_shape=jax.ShapeDtypeStruct((bsz, seq, B_HEADS * HEAD_DIM), BF16),
        scratch_shapes=[pltpu.VMEM((rows, 1), F32), pltpu.VMEM((rows, 1), F32), pltpu.VMEM((rows, LANES), F32)] * 2,
        compiler_params=_cparams(("parallel", "parallel", "arbitrary")),
        name="nsa",
    )(qbn, qbr, gsig, kcv, kvb, c2s, onehot)


R_EXP = 0
R_GRP = N_EXPERTS


def _merge_kernel(oa_ref, ob_ref, gate_ref, x_ref, wa_ref, wb_ref, wo_ref, g_ref, wr_ref, br_ref,
                  x1_ref, h2_ref, route_ref):
    ya = jnp.dot(oa_ref[...], wa_ref[...], preferred_element_type=F32)
    yb = jnp.dot(ob_ref[...], wb_ref[...], preferred_element_type=F32)
    merged = gate_ref[:, :D_MODEL].astype(F32) * ya + gate_ref[:, D_MODEL:].astype(F32) * yb
    x1 = x_ref[...] + jnp.dot(merged.astype(BF16), wo_ref[...], preferred_element_type=F32)
    x1_ref[...] = x1
    ms = jnp.mean(x1 * x1, axis=-1, keepdims=True)
    h2 = x1 * lax.rsqrt(ms + RMS_EPS) * g_ref[...]
    h2_ref[...] = h2

    logit = jnp.dot(h2, wr_ref[...], preferred_element_type=F32, precision=lax.Precision.HIGHEST) + br_ref[...]
    lane = lax.broadcasted_iota(I32, logit.shape, 1)
    lane_f = lane.astype(F32)

    def first_argmax(v):
        mx = jnp.max(v, axis=-1, keepdims=True)
        return mx, jnp.min(jnp.where(v == mx, lane_f, float(LANES)), axis=-1, keepdims=True)

    is_grp = (lane >= R_GRP) & (lane < R_GRP + N_GROUPS)
    gl = jnp.where(is_grp, logit, -jnp.inf)
    gmax, garg = first_argmax(gl)
    g_sel = (garg - R_GRP).astype(I32)
    g_prob = 1.0 / jnp.sum(jnp.where(is_grp, jnp.exp(gl - gmax), 0.0), axis=-1, keepdims=True)
    grp_of_lane = lax.shift_right_logical(lane, EXPERTS_PER_GROUP.bit_length() - 1)
    in_grp = (lane < N_EXPERTS) & (grp_of_lane == g_sel)
    el = jnp.where(in_grp, logit, -jnp.inf)
    v1, i1 = first_argmax(el)
    v2, i2 = first_argmax(jnp.where(lane_f == i1, -jnp.inf, el))
    e2 = jnp.exp(v2 - v1)
    w1 = g_prob / (1.0 + e2)
    w2 = g_prob * e2 / (1.0 + e2)
    route_ref[...] = jnp.where(lane == 0, i1, jnp.where(lane == 1, i2,
                               jnp.where(lane == 2, w1, jnp.where(lane == 3, w2, 0.0))))


def _merge(o_a, o_b, gates, x2, wa, wb, wo, g, wr, br, tm):
    n = x2.shape[0]
    row = lambda w: pl.BlockSpec((tm, w), lambda i: (i, 0))
    full = lambda a: pl.BlockSpec(a.shape, lambda i: (0, 0))
    return pl.pallas_call(
        _merge_kernel,
        grid=(n // tm,),
        in_specs=[row(512), row(512), row(2048), row(D_MODEL), full(wa), full(wb), full(wo), full(g), full(wr), full(br)],
        out_specs=[row(D_MODEL), row(D_MODEL), row(LANES)],
        out_shape=[jax.ShapeDtypeStruct((n, D_MODEL), F32), jax.ShapeDtypeStruct((n, D_MODEL), F32),
                   jax.ShapeDtypeStruct((n, LANES), F32)],
        compiler_params=_cparams(("parallel",)),
        name="merge",
    )(o_a, o_b, gates, x2, wa, wb, wo, g, wr, br)


def _expert_kernel(be_ref, slot_ref, h_hbm, wgu_ref, wd_ref, y_ref, buf, sem):
    i = pl.program_id(0)
    base = i * MOE_BLOCK

    def row_copy(r, tok):
        return pltpu.make_async_copy(h_hbm.at[pl.ds(tok, 1)], buf.at[pl.ds(r, 1)], sem)

    def issue(r, carry):
        row_copy(r, slot_ref[base + r]).start()
        return carry

    lax.fori_loop(0, MOE_BLOCK, issue, 0)

    def drain(r, carry):
        row_copy(r, 0).wait()
        return carry

    lax.fori_loop(0, MOE_BLOCK, drain, 0)
    xb = buf[...].astype(BF16)
    gu = jnp.dot(xb, wgu_ref[0], preferred_element_type=F32)
    gate, up = gu[:, :D_EXPERT], gu[:, D_EXPERT:]
    act = gate * jax.nn.sigmoid(gate) * up
    y_ref[...] = jnp.dot(act.astype(BF16), wd_ref[0], preferred_element_type=F32)


def _experts(blk_expert, slot_tok, h2, wgu, wd):
    n_blocks = blk_expert.shape[0]
    grid_spec = pltpu.PrefetchScalarGridSpec(
        num_scalar_prefetch=2,
        grid=(n_blocks,),
        in_specs=[pl.BlockSpec(memory_space=pl.ANY),
                  pl.BlockSpec((1, D_MODEL, 2 * D_EXPERT), lambda i, be, st: (be[i], 0, 0)),
                  pl.BlockSpec((1, D_EXPERT, D_MODEL), lambda i, be, st: (be[i], 0, 0))],
        out_specs=pl.BlockSpec((MOE_BLOCK, D_MODEL), lambda i, be, st: (i, 0)),
        scratch_shapes=[pltpu.VMEM((MOE_BLOCK, D_MODEL), F32), pltpu.SemaphoreType.DMA],
    )
    return pl.pallas_call(
        _expert_kernel,
        grid_spec=grid_spec,
        out_shape=jax.ShapeDtypeStruct((n_blocks * MOE_BLOCK, D_MODEL), F32),
        compiler_params=_cparams(("arbitrary",)),
        name="experts",
    )(blk_expert, slot_tok, h2, wgu, wd)


def _combine_kernel(dest_ref, y_hbm, x1_ref, route_ref, g_ref, o_ref, buf, sem, *, tm):
    i = pl.program_id(0)
    base = i * tm * 2

    def row_copy(r, src):
        return pltpu.make_async_copy(y_hbm.at[pl.ds(src, 1)], buf.at[pl.ds(r, 1)], sem)

    def issue(r, carry):
        row_copy(r, dest_ref[base + 2 * r]).start()
        row_copy(tm + r, dest_ref[base + 2 * r + 1]).start()
        return carry

    lax.fori_loop(0, tm, issue, 0)

    def drain(r, carry):
        row_copy(r, 0).wait()
        return carry

    lax.fori_loop(0, 2 * tm, drain, 0)
    route = route_ref[...]
    x = x1_ref[...] + route[:, 2:3] * buf[:tm] + route[:, 3:4] * buf[tm:]
    ms = jnp.mean(x * x, axis=-1, keepdims=True)
    o_ref[...] = x * lax.rsqrt(ms + RMS_EPS) * g_ref[...]


def _combine(dest, y_buf, x1, route, g, tm):
    n = x1.shape[0]
    grid_spec = pltpu.PrefetchScalarGridSpec(
        num_scalar_prefetch=1,
        grid=(n // tm,),
        in_specs=[pl.BlockSpec(memory_space=pl.ANY),
                  pl.BlockSpec((tm, D_MODEL), lambda i, d: (i, 0)),
                  pl.BlockSpec((tm, LANES), lambda i, d: (i, 0)),
                  pl.BlockSpec((1, D_MODEL), lambda i, d: (0, 0))],
        out_specs=pl.BlockSpec((tm, D_MODEL), lambda i, d: (i, 0)),
        scratch_shapes=[pltpu.VMEM((2 * tm, D_MODEL), F32), pltpu.SemaphoreType.DMA],
    )
    return pl.pallas_call(
        functools.partial(_combine_kernel, tm=tm),
        grid_spec=grid_spec,
        out_shape=jax.ShapeDtypeStruct((n, D_MODEL), F32),
        compiler_params=_cparams(("arbitrary",)),
        name="combine",
    )(dest, y_buf, x1, route, g)


def _dispatch_plan(expert):
    n_tok = expert.shape[0]
    n_asg = n_tok * 2
    e_flat = expert.reshape(n_asg)
    onehot = (e_flat[:, None] == jnp.arange(N_EXPERTS, dtype=I32)[None, :]).astype(I32)
    csum = jnp.cumsum(onehot, axis=0)
    rank = jnp.take_along_axis(csum, e_flat[:, None], axis=1)[:, 0] - 1
    counts = csum[-1]
    padded = (counts + MOE_BLOCK - 1) // MOE_BLOCK * MOE_BLOCK
    pad_end = jnp.cumsum(padded)
    pad_start = pad_end - padded
    dest = (pad_start[e_flat] + rank).astype(I32)
    n_blocks = -(-n_asg // MOE_BLOCK) + N_EXPERTS
    slot_tok = jnp.zeros((n_blocks * MOE_BLOCK,), I32).at[dest].set(jnp.arange(n_asg, dtype=I32) // 2)
    blk_expert = jnp.minimum(jnp.searchsorted(pad_end, jnp.arange(n_blocks, dtype=I32) * MOE_BLOCK, side='right'),
                             N_EXPERTS - 1).astype(I32)
    return dest, slot_tok, blk_expert


def _pick(n, prefs):
    for p in prefs:
        if n % p == 0:
            return p
    return n


def _layer(x, positions, norm_mix, w_in, pe_k, w1_k, w2_k, pe_v, w1_v, w2_v, w_br_a, w_br_b, w_out,
           norm_ffn, w_group, b_group, w_expert, b_expert, w_gate_up, w_down, out_gain):
    bsz, seq, d = x.shape
    n = bsz * seq
    x2 = x.reshape(n, d)
    tables = _rope_tables(positions)

    w_p = _permute_columns(w_in.astype(BF16))
    tm = _pick(n, (512, 256, 128))
    qa, kva, qi, kiw, qbn, qbr, kvb, cmp_, gsig, gates = _inproj(x2, norm_mix[None, :], w_p, tables, tm)

    n_grp = seq // CMP_STRIDE
    cmp4 = cmp_.reshape(bsz, seq, 4, HEAD_DIM).transpose(0, 2, 1, 3).reshape(bsz, 4, n_grp, CMP_STRIDE * HEAD_DIM)
    pe = jnp.stack([pe_k.reshape(1, -1), pe_v.reshape(1, -1)])
    kcvc = _compress(cmp4, pe, jnp.stack([w1_k, w1_v]).astype(BF16), jnp.stack([w2_k, w2_v]).astype(BF16))
    kcv = jnp.concatenate([kcvc[:, :B_KV_HEADS], kcvc[:, B_KV_HEADS:]], axis=-1)

    r3 = lambda a: a.reshape(bsz, seq, a.shape[-1])
    t = _pick(seq, (128,))
    ck = _pick(seq, (512,))
    o_a = _dsa(r3(qa), r3(qi), r3(kiw), r3(kva), t, ck)
    o_b = _nsa(r3(qbn), r3(qbr), r3(gsig), kcv, r3(kvb), t, ck, min(t, 128))

    wr = jnp.zeros((d, LANES), F32).at[:, R_EXP:R_EXP + N_EXPERTS].set(w_expert).at[:, R_GRP:R_GRP + N_GROUPS].set(w_group)
    br = jnp.zeros((1, LANES), F32).at[0, R_EXP:R_EXP + N_EXPERTS].set(b_expert).at[0, R_GRP:R_GRP + N_GROUPS].set(b_group)
    x1, h2, route = _merge(o_a.reshape(n, 512), o_b.reshape(n, 512), gates, x2,
                           w_br_a.astype(BF16), w_br_b.astype(BF16), w_out.astype(BF16),
                           norm_ffn[None, :], wr, br, _pick(n, (512, 256, 128)))

    expert = route[:, :2].astype(I32)
    dest, slot_tok, blk_expert = _dispatch_plan(expert)
    y_buf = _experts(blk_expert, slot_tok, h2, w_gate_up.astype(BF16), w_down.astype(BF16))
    out = _combine(dest, y_buf, x1, route, out_gain[None, :], _pick(n, (256, 128)))
    return out.reshape(bsz, seq, d)


def kernel(x, positions, norm_mix, w_in, pe_k, w1_k, w2_k, pe_v, w1_v, w2_v, w_br_a, w_br_b, w_out, norm_ffn,
           w_group, b_group, w_expert, b_expert, w_gate_up, w_down, norm_final):
    assert norm_mix.shape[0] == 1, "single-layer block"
    l = 0
    return _layer(x, positions, norm_mix[l], w_in[l], pe_k[l], w1_k[l], w2_k[l], pe_v[l], w1_v[l], w2_v[l],
                  w_br_a[l], w_br_b[l], w_out[l], norm_ffn[l], w_group[l], b_group[l], w_expert[l], b_expert[l],
                  w_gate_up[l], w_down[l], norm_final)
```

```python
import functools

import numpy as np
import jax
import jax.numpy as jnp
from jax import lax
from jax.experimental import pallas as pl
from jax.experimental.pallas import tpu as pltpu

D_MODEL = 1024
HEAD_DIM = 64
ROPE_THETA = 10000.0
RMS_EPS = 1e-6
A_HEADS = 8
IDX_HEADS = 8
IDX_DIM = 32
DSA_TOPK_MAX = 256
B_HEADS = 8
B_KV_HEADS = 2
B_GROUP = B_HEADS // B_KV_HEADS
CMP_LEN = 32
CMP_STRIDE = 16
CMP_HIDDEN = 128
SLC_LEN = 64
SLC_TOPN = 16
WINDOW = 512
N_GROUPS = 4
EXPERTS_PER_GROUP = 8
N_EXPERTS = N_GROUPS * EXPERTS_PER_GROUP
D_EXPERT = 256
MOE_BLOCK = 256

IN_SPLITS = (A_HEADS * HEAD_DIM, HEAD_DIM, HEAD_DIM, IDX_HEADS * IDX_DIM, IDX_DIM, IDX_HEADS,
             B_HEADS * HEAD_DIM, 6 * B_KV_HEADS * HEAD_DIM, 3 * B_HEADS, D_MODEL, D_MODEL)
IN_OFFS = tuple(int(v) for v in np.cumsum((0,) + IN_SPLITS)[:-1])

LANES = 128
MASK_NEG = -3e30
M_INIT = -1e30
VMEM_LIMIT = 48 * 1024 * 1024
SCALE = HEAD_DIM ** -0.5

F32 = jnp.float32
BF16 = jnp.bfloat16
I32 = jnp.int32


def _cparams(sem):
    return pltpu.CompilerParams(dimension_semantics=sem, vmem_limit_bytes=VMEM_LIMIT)


P_QA = 0
P_KVA = 512
P_QI = 640
P_KIW = 896
P_QB = 1024
P_KVB = 1536
P_CMP = 2048
P_GB = 2304
P_GATE = 2432
P_WIDTH = 4480


def _in_perm():
    idx = -np.ones((P_WIDTH,), np.int64)
    o_qa, o_ka, o_va, o_qi, o_ki, o_wi, o_qb, o_kvb, o_gb, o_ga, o_gbb = IN_OFFS
    idx[P_QA:P_QA + 512] = o_qa + np.arange(512)
    idx[P_KVA:P_KVA + 64] = o_ka + np.arange(64)
    idx[P_KVA + 64:P_KVA + 128] = o_va + np.arange(64)
    idx[P_QI:P_QI + 256] = o_qi + np.arange(256)
    idx[P_KIW:P_KIW + 32] = o_ki + np.arange(32)
    idx[P_KIW + 32:P_KIW + 40] = o_wi + np.arange(8)
    idx[P_QB:P_QB + 512] = o_qb + np.arange(512)

    def kvb_col(s, k):
        return o_kvb + s * (B_KV_HEADS * HEAD_DIM) + k * HEAD_DIM + np.arange(HEAD_DIM)

    for k in range(B_KV_HEADS):
        for j, s in enumerate((2, 3, 4, 5)):
            lo = P_KVB + k * 256 + j * 64
            idx[lo:lo + 64] = kvb_col(s, k)
    for j, (s, k) in enumerate(((0, 0), (0, 1), (1, 0), (1, 1))):
        lo = P_CMP + j * 64
        idx[lo:lo + 64] = kvb_col(s, k)
    idx[P_GB:P_GB + 24] = o_gb + np.arange(24)
    idx[P_GATE:P_GATE + 1024] = o_ga + np.arange(1024)
    idx[P_GATE + 1024:P_GATE + 2048] = o_gbb + np.arange(1024)
    return idx


_IN_PERM = _in_perm()


def _permute_columns(w):
    pieces, start = [], 0
    for pos in range(1, P_WIDTH + 1):
        if pos < P_WIDTH:
            prev, cur = int(_IN_PERM[pos - 1]), int(_IN_PERM[pos])
            same_run = (prev < 0 and cur < 0) or (prev >= 0 and cur == prev + 1)
        if pos == P_WIDTH or not same_run:
            src = int(_IN_PERM[start])
            pieces.append(jnp.zeros((w.shape[0], pos - start), w.dtype) if src < 0 else w[:, src:src + pos - start])
            start = pos
    return jnp.concatenate(pieces, axis=1)


def _trig_kernel(pos_ref, freq_ref, cos_ref, sin_ref):
    ang = pos_ref[...] * freq_ref[0]
    cos_ref[...] = jnp.cos(ang)
    sin_ref[...] = jnp.sin(ang)


def _rope_tables(positions):
    n = positions.size
    pos = positions.reshape(n).astype(F32)
    h64, h32 = HEAD_DIM // 2, IDX_DIM // 2
    f64 = ROPE_THETA ** (-jnp.arange(h64, dtype=F32) / h64)
    f32_ = ROPE_THETA ** (-jnp.arange(h32, dtype=F32) / h32)
    rows64, rows32 = n * h64 // LANES, n * h32 // LANES
    pos_rep = jnp.concatenate([jnp.repeat(pos, h64).reshape(rows64, LANES),
                               jnp.repeat(pos, h32).reshape(rows32, LANES)], axis=0)
    freq = jnp.stack([jnp.tile(f64, LANES // h64), jnp.tile(f32_, LANES // h32)])[:, None, :]
    rows = rows64 + rows32
    tr = rows32 if rows32 <= 1024 else 1024
    assert rows64 % tr == 0 and rows32 % tr == 0
    n64_tiles = rows64 // tr
    cos, sin = pl.pallas_call(
        _trig_kernel,
        grid=(rows // tr,),
        in_specs=[pl.BlockSpec((tr, LANES), lambda i: (i, 0)),
                  pl.BlockSpec((1, 1, LANES), lambda i: (jnp.where(i >= n64_tiles, 1, 0), 0, 0))],
        out_specs=[pl.BlockSpec((tr, LANES), lambda i: (i, 0))] * 2,
        out_shape=[jax.ShapeDtypeStruct((rows, LANES), F32)] * 2,
        compiler_params=_cparams(("parallel",)),
        name="trig",
    )(pos_rep, freq)
    c64 = cos[:rows64].reshape(n, h64)
    s64 = sin[:rows64].reshape(n, h64)
    c32 = cos[rows64:].reshape(n, h32)
    s32 = sin[rows64:].reshape(n, h32)
    cos64 = jnp.tile(c64, (1, LANES // h64))
    sin64 = jnp.tile(jnp.concatenate([-s64, s64], axis=1), (1, LANES // HEAD_DIM))
    cos32 = jnp.tile(c32, (1, LANES // h32))
    sin32 = jnp.tile(jnp.concatenate([-s32, s32], axis=1), (1, LANES // IDX_DIM))
    return cos64, sin64, cos32, sin32


def _rope_slab(x, cos, sin, half):
    lane = lax.broadcasted_iota(I32, x.shape, 1)
    first = (lane % (2 * half)) < half
    rot = jnp.where(first, pltpu.roll(x, LANES - half, 1), pltpu.roll(x, half, 1))
    return x * cos + rot * sin


def _inproj_kernel(x_ref, g_ref, w_ref, c64_ref, s64_ref, c32_ref, s32_ref,
                   qa_ref, kva_ref, qi_ref, kiw_ref, qbn_ref, qbr_ref, kvb_ref, cmp_ref, gsig_ref, gate_ref, *, seq):
    x = x_ref[...]
    ms = jnp.mean(x * x, axis=-1, keepdims=True)
    h = (x * lax.rsqrt(ms + RMS_EPS) * g_ref[...]).astype(BF16)
    c64, s64, c32, s32 = c64_ref[...], s64_ref[...], c32_ref[...], s32_ref[...]
    lane = lax.broadcasted_iota(I32, c64.shape, 1)

    def proj(lo, width):
        return jnp.dot(h, w_ref[:, lo:lo + width], preferred_element_type=F32)

    def slab(p, j):
        return p[:, j * LANES:(j + 1) * LANES]

    p = proj(P_QA, 512)
    for j in range(4):
        qa_ref[:, j * LANES:(j + 1) * LANES] = (_rope_slab(slab(p, j), c64, s64, 32) * SCALE).astype(BF16)
    p = proj(P_KVA, 128)
    kva_ref[...] = jnp.where(lane < HEAD_DIM, _rope_slab(p, c64, s64, 32), p).astype(BF16)
    p = proj(P_QI, 256)
    for j in range(2):
        qi_ref[:, j * LANES:(j + 1) * LANES] = _rope_slab(slab(p, j), c32, s32, 16).astype(BF16)
    p = proj(P_KIW, 128)
    kiw_ref[...] = jnp.where(lane < IDX_DIM, _rope_slab(p, c32, s32, 16), p)
    p = proj(P_QB, 512)
    qbn_ref[...] = (p * SCALE).astype(BF16)
    for j in range(4):
        qbr_ref[:, j * LANES:(j + 1) * LANES] = (_rope_slab(slab(p, j), c64, s64, 32) * SCALE).astype(BF16)
    p = proj(P_KVB, 512)
    tm = x.shape[0]
    pos = (pl.program_id(0) * tm) % seq + lax.broadcasted_iota(I32, c64.shape, 0)
    blk_hot = jnp.where(lane - HEAD_DIM == lax.shift_right_logical(pos, SLC_LEN.bit_length() - 1), 1.0, 0.0)
    for k in range(B_KV_HEADS):
        slc, win = slab(p, 2 * k), slab(p, 2 * k + 1)
        base = 3 * k * LANES
        kvb_ref[:, base:base + LANES] = jnp.where(lane < HEAD_DIM, _rope_slab(slc, c64, s64, 32), blk_hot).astype(BF16)
        kvb_ref[:, base + LANES:base + 2 * LANES] = jnp.where(lane < HEAD_DIM, _rope_slab(win, c64, s64, 32), win).astype(BF16)
        kvb_ref[:, base + 2 * LANES:base + 3 * LANES] = jnp.where(lane < HEAD_DIM, 0.0, slc).astype(BF16)
    cmp_ref[...] = proj(P_CMP, 256)
    gsig_ref[...] = jax.nn.sigmoid(proj(P_GB, 128))
    for j in range(2):
        gate_ref[:, j * 1024:(j + 1) * 1024] = jax.nn.sigmoid(proj(P_GATE + j * 1024, 1024)).astype(BF16)


def _inproj(x2, g, w_p, tables, tm, seq):
    n = x2.shape[0]
    assert seq % tm == 0
    row = lambda w: pl.BlockSpec((tm, w), lambda i: (i, 0))
    full = lambda a: pl.BlockSpec(a.shape, lambda i: (0, 0))
    outs = [(512, BF16), (128, BF16), (256, BF16), (128, F32), (512, BF16), (512, BF16), (3 * LANES * B_KV_HEADS, BF16),
            (256, F32), (128, F32), (2048, BF16)]
    return pl.pallas_call(
        functools.partial(_inproj_kernel, seq=seq),
        grid=(n // tm,),
        in_specs=[row(D_MODEL), full(g), full(w_p)] + [row(LANES)] * 4,
        out_specs=[row(w) for w, _ in outs],
        out_shape=[jax.ShapeDtypeStruct((n, w), dt) for w, dt in outs],
        compiler_params=_cparams(("parallel",)),
        name="inproj",
    )(x2, g, w_p, *tables)


def _compress_kernel(g_ref, pe_ref, w1_ref, w2_ref, o_ref):
    g = g_ref[0, 0]
    half = CMP_STRIDE * HEAD_DIM
    a = jnp.dot((g + pe_ref[0, :, :half]).astype(BF16), w1_ref[0, :half, :], preferred_element_type=F32)
    b = jnp.dot((g + pe_ref[0, :, half:]).astype(BF16), w1_ref[0, half:, :], preferred_element_type=F32)
    n_grp = g.shape[0]
    hid = a + pltpu.roll(b, n_grp - 1, 0)
    hid = hid * jax.nn.sigmoid(hid)
    o_ref[0, 0] = jnp.dot(hid.astype(BF16), w2_ref[0], preferred_element_type=F32).astype(BF16)


def _compress(cmp4, pe, w1, w2):
    bsz, _, n_grp, width = cmp4.shape
    return pl.pallas_call(
        _compress_kernel,
        grid=(bsz, 4),
        in_specs=[pl.BlockSpec((1, 1, n_grp, width), lambda b, a: (b, a, 0, 0)),
                  pl.BlockSpec((1, 1, 2 * width), lambda b, a: (a // 2, 0, 0)),
                  pl.BlockSpec((1, 2 * width, CMP_HIDDEN), lambda b, a: (a // 2, 0, 0)),
                  pl.BlockSpec((1, CMP_HIDDEN, HEAD_DIM), lambda b, a: (a // 2, 0, 0))],
        out_specs=pl.BlockSpec((1, 1, n_grp, HEAD_DIM), lambda b, a: (b, a, 0, 0)),
        out_shape=jax.ShapeDtypeStruct((bsz, 4, n_grp, HEAD_DIM), BF16),
        compiler_params=_cparams(("parallel", "parallel")),
        name="compress",
    )(cmp4, pe, w1, w2)


def _reset(m_ref, l_ref, acc_ref):
    m_ref[...] = jnp.full(m_ref.shape, M_INIT, F32)
    l_ref[...] = jnp.zeros(l_ref.shape, F32)
    acc_ref[...] = jnp.zeros(acc_ref.shape, F32)


def _stack_heads_t(q, n_heads):
    t = q.shape[0]
    lane = lax.broadcasted_iota(I32, (t, LANES), 1)
    parts = []
    for h in range(n_heads):
        slab = q[:, (h // 2) * LANES:(h // 2 + 1) * LANES].astype(F32)
        if h % 2:
            slab = pltpu.roll(slab, HEAD_DIM, 1)
        parts.append(jnp.where(lane < HEAD_DIM, slab, 0.0).T)
    return jnp.concatenate(parts, axis=1).astype(BF16)


def _unstack_heads_t(o_t, n_heads, t):
    lane = lax.broadcasted_iota(I32, (t, LANES), 1)
    slabs = []
    for j in range(n_heads // 2):
        even = pltpu.roll(o_t[:, (2 * j) * t:(2 * j + 1) * t].T, HEAD_DIM, 1)
        slabs.append(jnp.where(lane < HEAD_DIM, even, o_t[:, (2 * j + 1) * t:(2 * j + 2) * t].T))
    return jnp.concatenate(slabs, axis=1)


def _tn_dot(a, b):
    return lax.dot_general(a, b, (((0,), (0,)), ((), ())), preferred_element_type=F32)


def _online_update_t(s_t, bias_t, kv, n_heads, t, m_ref, l_ref, acc_ref):
    ps, alphas = [], []
    for h in range(n_heads):
        cols = slice(h * t, (h + 1) * t)
        sh = s_t[:, cols] if bias_t is None else s_t[:, cols] + bias_t
        m_old = m_ref[:, cols]
        m_new = jnp.maximum(m_old, jnp.max(sh, axis=0, keepdims=True))
        alpha = jnp.exp(m_old - m_new)
        p = jnp.exp(sh - m_new)
        l_ref[:, cols] = alpha * l_ref[:, cols] + jnp.sum(p, axis=0, keepdims=True)
        m_ref[:, cols] = m_new
        ps.append(p.astype(BF16))
        alphas.append(alpha)
    p_t = jnp.concatenate(ps, axis=1)
    alpha_all = jnp.concatenate(alphas, axis=1)
    acc_ref[...] = alpha_all * acc_ref[...] + _tn_dot(kv, p_t)


INT_MIN = -2 ** 31


def _dsa_kernel(qa_ref, qi_ref, wq_ref, kva_ref, kiw_ref, o_ref,
                key_scr, m_scr, l_scr, acc_scr, *, t, ck, k_sel):
    i = pl.program_id(1)
    t0 = i * t
    n_chunks = (t0 + t + ck - 1) // ck
    qpos = t0 + lax.broadcasted_iota(I32, (1, t), 1)
    sub_k = lax.broadcasted_iota(I32, (ck, t), 0)

    qi_t = qi_ref[0].astype(F32).T
    w_t = wq_ref[0].T
    zpad = jnp.zeros((LANES - IDX_DIM, t), F32)
    qi_all = jnp.concatenate([jnp.concatenate([qi_t[h * IDX_DIM:(h + 1) * IDX_DIM], zpad], axis=0)
                              for h in range(IDX_HEADS)], axis=1).astype(BF16)

    def score_chunk(c, carry):
        k0 = pl.multiple_of(c * ck, ck)
        kic = kiw_ref[0, pl.ds(k0, ck), :].astype(BF16)
        lg = jnp.dot(kic, qi_all, preferred_element_type=F32)
        acc = jnp.zeros((ck, t), F32)
        for h in range(IDX_HEADS):
            acc = acc + w_t[IDX_DIM + h:IDX_DIM + h + 1] * jnp.maximum(lg[:, h * t:(h + 1) * t], 0.0)
        key_scr[pl.ds(k0, ck), :] = jnp.where(k0 + sub_k <= qpos, acc, -jnp.inf)
        return carry

    lax.fori_loop(0, n_chunks, score_chunk, 0)

    n_part = 8

    def count_where(pred):
        def body(c, cnt):
            k0 = pl.multiple_of(c * ck, ck)
            hit = jnp.where(pred(key_scr[pl.ds(k0, ck), :], k0), 1.0, 0.0)
            return cnt + jnp.sum(hit.reshape(n_part, ck // (8 * n_part), 8, t), axis=1)
        cnt = lax.fori_loop(0, n_chunks, body, jnp.zeros((n_part, 8, t), F32))
        return jnp.sum(jnp.sum(cnt, axis=0), axis=0, keepdims=True)

    def as_score(code):
        return pltpu.bitcast(jnp.where(code < 0, code ^ 0x7FFFFFFF, code), F32)

    def bit_step(b, lo):
        cand = lo + lax.shift_left(jnp.int32(1), 31 - b)
        v = as_score(cand)
        return jnp.where(count_where(lambda sc, k0: sc >= v) >= k_sel, cand, lo)

    code = lax.fori_loop(0, 32, bit_step, jnp.full((1, t), INT_MIN, I32))
    thr = jnp.where(code == INT_MIN, -jnp.inf, as_score(code))
    n_gt = count_where(lambda sc, k0: sc > thr)
    n_ge = count_where(lambda sc, k0: sc >= thr)
    need = k_sel - n_gt

    def tie_bound():
        n_bits = int(key_scr.shape[0]).bit_length()

        def step(b, j):
            cand = j + lax.shift_left(jnp.int32(1), n_bits - 1 - b)
            below = count_where(lambda sc, k0: (sc == thr) & (k0 + sub_k < cand))
            return jnp.where(below <= need, cand, j)
        return lax.fori_loop(0, n_bits, step, jnp.zeros((1, t), I32))

    has_excess = jnp.max(jnp.where(n_ge > k_sel, 1.0, 0.0)) > 0.0
    jbound = lax.cond(has_excess, tie_bound, lambda: jnp.full((1, t), 2 ** 30, I32))

    q_t = _stack_heads_t(qa_ref[0], A_HEADS)
    _reset(m_scr, l_scr, acc_scr)

    def attn_chunk(c, carry):
        k0 = pl.multiple_of(c * ck, ck)
        key = key_scr[pl.ds(k0, ck), :]
        idx = k0 + sub_k
        sel = ((key > thr) | ((key == thr) & (idx < jbound))) & (idx <= qpos)
        bias_t = jnp.where(sel, 0.0, MASK_NEG)
        kv = kva_ref[0, pl.ds(k0, ck), :]
        s_t = jnp.dot(kv, q_t, preferred_element_type=F32)
        _online_update_t(s_t, bias_t, kv, A_HEADS, t, m_scr, l_scr, acc_scr)
        return carry

    lax.fori_loop(0, n_chunks, attn_chunk, 0)
    o_t = acc_scr[...] / l_scr[...]
    o_ref[0] = _unstack_heads_t(o_t, A_HEADS, t).astype(BF16)


def _dsa(qa, qi, kiw, kva, t, ck):
    bsz, seq, _ = qa.shape
    k_sel = min(DSA_TOPK_MAX, seq // 4)
    kern = functools.partial(_dsa_kernel, t=t, ck=ck, k_sel=k_sel)
    blk = lambda w: pl.BlockSpec((1, t, w), lambda b, i: (b, i, 0))
    res = lambda w: pl.BlockSpec((1, seq, w), lambda b, i: (b, 0, 0))
    cols = A_HEADS * t
    return pl.pallas_call(
        kern,
        grid=(bsz, seq // t),
        in_specs=[blk(512), blk(256), blk(LANES), res(LANES), res(LANES)],
        out_specs=blk(512),
        out_shape=jax.ShapeDtypeStruct((bsz, seq, 512), BF16),
        scratch_shapes=[pltpu.VMEM((seq, t), F32), pltpu.VMEM((1, cols), F32), pltpu.VMEM((1, cols), F32),
                        pltpu.VMEM((LANES, cols), F32)],
        compiler_params=_cparams(("parallel", "arbitrary")),
        name="dsa",
    )(qa, qi, kiw, kva, kiw)


def _nsa_kernel(qn_ref, qr_ref, g_ref, kcv_ref, kv_ref, c2s_ref, o_ref,
                m_scr, l_scr, acc_scr, *, t, ck, n_slc, n_sel):
    kh = pl.program_id(1)
    i = pl.program_id(2)
    t0 = i * t
    qpos = t0 + lax.broadcasted_iota(I32, (1, t), 1)
    qn_t = _stack_heads_t(qn_ref[0], B_GROUP)
    qr_t = _stack_heads_t(qr_ref[0], B_GROUP)
    heads = [slice(g * t, (g + 1) * t) for g in range(B_GROUP)]

    kcv = kcv_ref[0, 0]
    n_cp = kcv.shape[0]
    s_t = jnp.dot(kcv, qn_t, preferred_element_type=F32)
    cmp_end = lax.broadcasted_iota(I32, (n_cp, t), 0) * CMP_STRIDE + (CMP_LEN - 1)
    vis = cmp_end <= qpos
    probs = []
    for cols in heads:
        sh = jnp.where(vis, s_t[:, cols], M_INIT)
        p = jnp.where(vis, jnp.exp(sh - jnp.max(sh, axis=0, keepdims=True)), 0.0)
        den = jnp.sum(p, axis=0, keepdims=True)
        probs.append(p * jnp.where(den > 0.0, 1.0 / den, 0.0))
    o_cmp_t = _tn_dot(kcv, jnp.concatenate(probs, axis=1).astype(BF16))

    psum = probs[0]
    for g in range(1, B_GROUP):
        psum = psum + probs[g]
    p_hi = psum.astype(BF16)
    p_lo = (psum - p_hi.astype(F32)).astype(BF16)
    imp = (jnp.dot(c2s_ref[...], p_hi, preferred_element_type=F32)
           + jnp.dot(c2s_ref[...], p_lo, preferred_element_type=F32))
    jx = lax.broadcasted_iota(I32, (n_slc, t), 0)
    admiss = jx * SLC_LEN <= qpos
    cur = lax.shift_right_logical(qpos, SLC_LEN.bit_length() - 1)
    forced = admiss & ((jx == 0) | (jx == cur) | (jx == cur - 1))
    score = jnp.where(forced, jnp.inf, jnp.where(admiss, imp, -jnp.inf))
    rank = jnp.zeros((n_slc, t), F32)
    for j2 in range(n_slc):
        row = score[j2:j2 + 1, :]
        ahead = (row > score) | ((row == score) & (j2 < jx))
        rank = rank + jnp.where(ahead, 1.0, 0.0)
    selbias = jnp.where(rank < n_sel, 0.0, MASK_NEG)
    if n_slc < HEAD_DIM:
        selbias = jnp.concatenate([selbias, jnp.zeros((HEAD_DIM - n_slc, t), F32)], axis=0)
    q_aug_t = jnp.concatenate([qr_t[:HEAD_DIM], jnp.concatenate([selbias.astype(BF16)] * B_GROUP, axis=1)], axis=0)

    _reset(m_scr, l_scr, acc_scr)
    sub_k = lax.broadcasted_iota(I32, (ck, t), 0)

    def slc_chunk(c, causal):
        k0 = pl.multiple_of(c * ck, ck)
        s_c = jnp.dot(kv_ref[0, pl.ds(k0, ck), :LANES], q_aug_t, preferred_element_type=F32)
        bias_t = jnp.where(k0 + sub_k <= qpos, 0.0, MASK_NEG) if causal else None
        _online_update_t(s_c, bias_t, kv_ref[0, pl.ds(k0, ck), 2 * LANES:], B_GROUP, t, m_scr, l_scr, acc_scr)

    c_diag = t0 // ck

    def full_chunk(c, carry):
        slc_chunk(c, False)
        return carry

    lax.fori_loop(0, c_diag, full_chunk, 0)
    slc_chunk(c_diag, True)
    o_slc_t = acc_scr[...] / l_scr[...]

    ww = WINDOW + t
    w0 = pl.multiple_of(jnp.maximum(t0 - WINDOW, 0), t)
    kvw = kv_ref[0, pl.ds(w0, ww), LANES:2 * LANES]
    s_w = jnp.dot(kvw, qr_t, preferred_element_type=F32)
    rel = qpos - (w0 + lax.broadcasted_iota(I32, (ww, t), 0))
    bias_w = jnp.where((rel >= 0) & (rel < WINDOW), 0.0, MASK_NEG)
    pws, dens = [], []
    for cols in heads:
        sh = s_w[:, cols] + bias_w
        p = jnp.exp(sh - jnp.max(sh, axis=0, keepdims=True))
        dens.append(jnp.sum(p, axis=0, keepdims=True))
        pws.append(p.astype(BF16))
    o_win_t = _tn_dot(kvw, jnp.concatenate(pws, axis=1)) / jnp.concatenate(dens, axis=1)

    g_t = g_ref[0].T
    grow = lax.broadcasted_iota(I32, g_t.shape, 0)
    outs = []
    for g, cols in enumerate(heads):
        hd = kh * B_GROUP + g

        def gate(branch):
            return jnp.sum(jnp.where(grow == branch * B_HEADS + hd, g_t, 0.0), axis=0, keepdims=True)

        outs.append(gate(0) * o_cmp_t[:, cols] + gate(1) * o_slc_t[:, cols] + gate(2) * o_win_t[:, cols])
    o_ref[0] = _unstack_heads_t(jnp.concatenate(outs, axis=1), B_GROUP, t).astype(BF16)


def _nsa(qbn, qbr, gsig, kcv, kvb, t, ck):
    bsz, seq, _ = qbn.shape
    n_cp = kcv.shape[2]
    n_slc = seq // SLC_LEN
    n_sel = min(SLC_TOPN, n_slc)
    assert n_slc <= HEAD_DIM and seq >= WINDOW + t and ck % t == 0
    n = np.arange(n_cp)
    js = np.arange(n_slc) * SLC_LEN
    overlap = (n[None, :] * CMP_STRIDE < js[:, None] + SLC_LEN) & (n[None, :] * CMP_STRIDE + CMP_LEN > js[:, None])
    overlap &= (n[None, :] < n_cp - 1)
    c2s_t = jnp.asarray(overlap.astype(np.float32), BF16)
    kern = functools.partial(_nsa_kernel, t=t, ck=ck, n_slc=n_slc, n_sel=n_sel)
    cols = B_GROUP * t
    width = B_GROUP * HEAD_DIM
    qblk = pl.BlockSpec((1, t, width), lambda b, k, i: (b, i, k))
    return pl.pallas_call(
        kern,
        grid=(bsz, B_KV_HEADS, seq // t),
        in_specs=[qblk, qblk,
                  pl.BlockSpec((1, t, LANES), lambda b, k, i: (b, i, 0)),
                  pl.BlockSpec((1, 1, n_cp, LANES), lambda b, k, i: (b, k, 0, 0)),
                  pl.BlockSpec((1, seq, 3 * LANES), lambda b, k, i: (b, 0, k)),
                  pl.BlockSpec((n_slc, n_cp), lambda b, k, i: (0, 0))],
        out_specs=qblk,
        out_shape=jax.ShapeDtypeStruct((bsz, seq, B_HEADS * HEAD_DIM), BF16),
        scratch_shapes=[pltpu.VMEM((1, cols), F32), pltpu.VMEM((1, cols), F32), pltpu.VMEM((LANES, cols), F32)],
        compiler_params=_cparams(("parallel", "parallel", "arbitrary")),
        name="nsa",
    )(qbn, qbr, gsig, kcv, kvb, c2s_t)


R_EXP = 0
R_GRP = N_EXPERTS


def _merge_kernel(oa_ref, ob_ref, gate_ref, x_ref, wa_ref, wb_ref, wo_ref, g_ref, wr_ref, br_ref,
                  x1_ref, h2_ref, route_ref):
    ya = jnp.dot(oa_ref[...], wa_ref[...], preferred_element_type=F32)
    yb = jnp.dot(ob_ref[...], wb_ref[...], preferred_element_type=F32)
    merged = gate_ref[:, :D_MODEL].astype(F32) * ya + gate_ref[:, D_MODEL:].astype(F32) * yb
    x1 = x_ref[...] + jnp.dot(merged.astype(BF16), wo_ref[...], preferred_element_type=F32)
    x1_ref[...] = x1
    ms = jnp.mean(x1 * x1, axis=-1, keepdims=True)
    h2 = x1 * lax.rsqrt(ms + RMS_EPS) * g_ref[...]
    h2_ref[...] = h2

    logit = jnp.dot(h2, wr_ref[...], preferred_element_type=F32, precision=lax.Precision.HIGHEST) + br_ref[...]
    lane = lax.broadcasted_iota(I32, logit.shape, 1)
    lane_f = lane.astype(F32)

    def first_argmax(v):
        mx = jnp.max(v, axis=-1, keepdims=True)
        return mx, jnp.min(jnp.where(v == mx, lane_f, float(LANES)), axis=-1, keepdims=True)

    is_grp = (lane >= R_GRP) & (lane < R_GRP + N_GROUPS)
    gl = jnp.where(is_grp, logit, -jnp.inf)
    gmax, garg = first_argmax(gl)
    g_sel = (garg - R_GRP).astype(I32)
    g_prob = 1.0 / jnp.sum(jnp.where(is_grp, jnp.exp(gl - gmax), 0.0), axis=-1, keepdims=True)
    grp_of_lane = lax.shift_right_logical(lane, EXPERTS_PER_GROUP.bit_length() - 1)
    in_grp = (lane < N_EXPERTS) & (grp_of_lane == g_sel)
    el = jnp.where(in_grp, logit, -jnp.inf)
    v1, i1 = first_argmax(el)
    v2, i2 = first_argmax(jnp.where(lane_f == i1, -jnp.inf, el))
    e2 = jnp.exp(v2 - v1)
    w1 = g_prob / (1.0 + e2)
    w2 = g_prob * e2 / (1.0 + e2)
    route_ref[...] = jnp.where(lane == 0, i1, jnp.where(lane == 1, i2,
                               jnp.where(lane == 2, w1, jnp.where(lane == 3, w2, 0.0))))


def _merge(o_a, o_b, gates, x2, wa, wb, wo, g, wr, br, tm):
    n = x2.shape[0]
    row = lambda w: pl.BlockSpec((tm, w), lambda i: (i, 0))
    full = lambda a: pl.BlockSpec(a.shape, lambda i: (0, 0))
    return pl.pallas_call(
        _merge_kernel,
        grid=(n // tm,),
        in_specs=[row(512), row(512), row(2048), row(D_MODEL), full(wa), full(wb), full(wo), full(g), full(wr), full(br)],
        out_specs=[row(D_MODEL), row(D_MODEL), row(LANES)],
        out_shape=[jax.ShapeDtypeStruct((n, D_MODEL), F32), jax.ShapeDtypeStruct((n, D_MODEL), F32),
                   jax.ShapeDtypeStruct((n, LANES), F32)],
        compiler_params=_cparams(("parallel",)),
        name="merge",
    )(o_a, o_b, gates, x2, wa, wb, wo, g, wr, br)


def _expert_kernel(be_ref, slot_ref, h_hbm, wgu_ref, wd_ref, y_ref, buf, sem):
    i = pl.program_id(0)

    def gather(block, slot):
        base = block * MOE_BLOCK

        def issue(r, carry):
            pltpu.make_async_copy(h_hbm.at[pl.ds(slot_ref[base + r], 1)], buf.at[slot, pl.ds(r, 1)],
                                  sem.at[slot]).start()
            return carry

        lax.fori_loop(0, MOE_BLOCK, issue, 0, unroll=8)

    @pl.when(i == 0)
    def _():
        gather(0, 0)

    @pl.when(i + 1 < pl.num_programs(0))
    def _():
        gather(i + 1, (i + 1) % 2)

    slot = i % 2
    pltpu.make_async_copy(h_hbm.at[pl.ds(0, MOE_BLOCK)], buf.at[slot], sem.at[slot]).wait()
    xb = buf[slot].astype(BF16)
    gu = jnp.dot(xb, wgu_ref[0], preferred_element_type=F32)
    gate, up = gu[:, :D_EXPERT], gu[:, D_EXPERT:]
    act = gate * jax.nn.sigmoid(gate) * up
    y_ref[...] = jnp.dot(act.astype(BF16), wd_ref[0], preferred_element_type=F32)


def _experts(blk_expert, slot_tok, h2, wgu, wd):
    n_blocks = blk_expert.shape[0]
    grid_spec = pltpu.PrefetchScalarGridSpec(
        num_scalar_prefetch=2,
        grid=(n_blocks,),
        in_specs=[pl.BlockSpec(memory_space=pl.ANY),
                  pl.BlockSpec((1, D_MODEL, 2 * D_EXPERT), lambda i, be, st: (be[i], 0, 0)),
                  pl.BlockSpec((1, D_EXPERT, D_MODEL), lambda i, be, st: (be[i], 0, 0))],
        out_specs=pl.BlockSpec((MOE_BLOCK, D_MODEL), lambda i, be, st: (i, 0)),
        scratch_shapes=[pltpu.VMEM((2, MOE_BLOCK, D_MODEL), F32), pltpu.SemaphoreType.DMA((2,))],
    )
    return pl.pallas_call(
        _expert_kernel,
        grid_spec=grid_spec,
        out_shape=jax.ShapeDtypeStruct((n_blocks * MOE_BLOCK, D_MODEL), F32),
        compiler_params=_cparams(("arbitrary",)),
        name="experts",
    )(blk_expert, slot_tok, h2, wgu, wd)


def _combine_kernel(dest_ref, y_hbm, x1_ref, route_ref, g_ref, o_ref, buf, sem, *, tm):
    i = pl.program_id(0)

    def gather(tile, slot):
        base = tile * tm * 2

        def issue(r, carry):
            for k in range(2):
                pltpu.make_async_copy(y_hbm.at[pl.ds(dest_ref[base + 2 * r + k], 1)],
                                      buf.at[slot, pl.ds(k * tm + r, 1)], sem.at[slot]).start()
            return carry

        lax.fori_loop(0, tm, issue, 0, unroll=4)

    @pl.when(i == 0)
    def _():
        gather(0, 0)

    @pl.when(i + 1 < pl.num_programs(0))
    def _():
        gather(i + 1, (i + 1) % 2)

    slot = i % 2
    pltpu.make_async_copy(y_hbm.at[pl.ds(0, 2 * tm)], buf.at[slot], sem.at[slot]).wait()
    route = route_ref[...]
    x = x1_ref[...] + route[:, 2:3] * buf[slot, :tm] + route[:, 3:4] * buf[slot, tm:]
    ms = jnp.mean(x * x, axis=-1, keepdims=True)
    o_ref[...] = x * lax.rsqrt(ms + RMS_EPS) * g_ref[...]


def _combine(dest, y_buf, x1, route, g, tm):
    n = x1.shape[0]
    grid_spec = pltpu.PrefetchScalarGridSpec(
        num_scalar_prefetch=1,
        grid=(n // tm,),
        in_specs=[pl.BlockSpec(memory_space=pl.ANY),
                  pl.BlockSpec((tm, D_MODEL), lambda i, d: (i, 0)),
                  pl.BlockSpec((tm, LANES), lambda i, d: (i, 0)),
                  pl.BlockSpec((1, D_MODEL), lambda i, d: (0, 0))],
        out_specs=pl.BlockSpec((tm, D_MODEL), lambda i, d: (i, 0)),
        scratch_shapes=[pltpu.VMEM((2, 2 * tm, D_MODEL), F32), pltpu.SemaphoreType.DMA((2,))],
    )
    return pl.pallas_call(
        functools.partial(_combine_kernel, tm=tm),
        grid_spec=grid_spec,
        out_shape=jax.ShapeDtypeStruct((n, D_MODEL), F32),
        compiler_params=_cparams(("arbitrary",)),
        name="combine",
    )(dest, y_buf, x1, route, g)


def _dispatch_plan(expert):
    n_tok = expert.shape[0]
    n_asg = n_tok * 2
    e_flat = expert.reshape(n_asg)
    onehot = (e_flat[:, None] == jnp.arange(N_EXPERTS, dtype=I32)[None, :]).astype(I32)
    csum = jnp.cumsum(onehot, axis=0)
    rank = jnp.take_along_axis(csum, e_flat[:, None], axis=1)[:, 0] - 1
    counts = csum[-1]
    padded = (counts + MOE_BLOCK - 1) // MOE_BLOCK * MOE_BLOCK
    pad_end = jnp.cumsum(padded)
    pad_start = pad_end - padded
    dest = (pad_start[e_flat] + rank).astype(I32)
    n_blocks = -(-n_asg // MOE_BLOCK) + N_EXPERTS
    slot_tok = jnp.zeros((n_blocks * MOE_BLOCK,), I32).at[dest].set(jnp.arange(n_asg, dtype=I32) // 2)
    blk_start = jnp.arange(n_blocks, dtype=I32) * MOE_BLOCK
    blk_expert = jnp.minimum(jnp.sum((pad_end[None, :] <= blk_start[:, None]).astype(I32), axis=1), N_EXPERTS - 1)
    return dest, slot_tok, blk_expert


def _pick(n, prefs):
    for p in prefs:
        if n % p == 0:
            return p
    return n


def _layer(x, positions, norm_mix, w_in, pe_k, w1_k, w2_k, pe_v, w1_v, w2_v, w_br_a, w_br_b, w_out,
           norm_ffn, w_group, b_group, w_expert, b_expert, w_gate_up, w_down, out_gain):
    bsz, seq, d = x.shape
    n = bsz * seq
    x2 = x.reshape(n, d)
    tables = _rope_tables(positions)

    w_p = _permute_columns(w_in.astype(BF16))
    tm = _pick(n, (512, 256, 128))
    qa, kva, qi, kiw, qbn, qbr, kvb, cmp_, gsig, gates = _inproj(x2, norm_mix[None, :], w_p, tables, tm, seq)

    n_grp = seq // CMP_STRIDE
    cmp4 = cmp_.reshape(bsz, seq, 4, HEAD_DIM).transpose(0, 2, 1, 3).reshape(bsz, 4, n_grp, CMP_STRIDE * HEAD_DIM)
    pe = jnp.stack([pe_k.reshape(1, -1), pe_v.reshape(1, -1)])
    kcvc = _compress(cmp4, pe, jnp.stack([w1_k, w1_v]).astype(BF16), jnp.stack([w2_k, w2_v]).astype(BF16))
    kcv = jnp.concatenate([kcvc[:, :B_KV_HEADS], kcvc[:, B_KV_HEADS:]], axis=-1)

    r3 = lambda a: a.reshape(bsz, seq, a.shape[-1])
    t = _pick(seq, (128,))
    ck = _pick(seq, (512,))
    o_a = _dsa(r3(qa), r3(qi), r3(kiw), r3(kva), t, ck)
    o_b = _nsa(r3(qbn), r3(qbr), r3(gsig), kcv, r3(kvb), t, ck)

    wr = jnp.zeros((d, LANES), F32).at[:, R_EXP:R_EXP + N_EXPERTS].set(w_expert).at[:, R_GRP:R_GRP + N_GROUPS].set(w_group)
    br = jnp.zeros((1, LANES), F32).at[0, R_EXP:R_EXP + N_EXPERTS].set(b_expert).at[0, R_GRP:R_GRP + N_GROUPS].set(b_group)
    x1, h2, route = _merge(o_a.reshape(n, 512), o_b.reshape(n, 512), gates, x2,
                           w_br_a.astype(BF16), w_br_b.astype(BF16), w_out.astype(BF16),
                           norm_ffn[None, :], wr, br, _pick(n, (512, 256, 128)))

    expert = route[:, :2].astype(I32)
    dest, slot_tok, blk_expert = _dispatch_plan(expert)
    y_buf = _experts(blk_expert, slot_tok, h2, w_gate_up.astype(BF16), w_down.astype(BF16))
    out = _combine(dest, y_buf, x1, route, out_gain[None, :], _pick(n, (256, 128)))
    return out.reshape(bsz, seq, d)


def kernel(x, positions, norm_mix, w_in, pe_k, w1_k, w2_k, pe_v, w1_v, w2_v, w_br_a, w_br_b, w_out, norm_ffn,
           w_group, b_group, w_expert, b_expert, w_gate_up, w_down, norm_final):
    assert norm_mix.shape[0] == 1, "single-layer block"
    l = 0
    return _layer(x, positions, norm_mix[l], w_in[l], pe_k[l], w1_k[l], w2_k[l], pe_v[l], w1_v[l], w2_v[l],
                  w_br_a[l], w_br_b[l], w_out[l], norm_ffn[l], w_group[l], b_group[l], w_expert[l], b_expert[l],
                  w_gate_up[l], w_down[l], norm_final)
```

```python
import functools

import numpy as np
import jax
import jax.numpy as jnp
from jax import lax
from jax.experimental import pallas as pl
from jax.experimental.pallas import tpu as pltpu

D_MODEL = 1024
HEAD_DIM = 64
ROPE_THETA = 10000.0
RMS_EPS = 1e-6
A_HEADS = 8
IDX_HEADS = 8
IDX_DIM = 32
DSA_TOPK_MAX = 256
B_HEADS = 8
B_KV_HEADS = 2
B_GROUP = B_HEADS // B_KV_HEADS
CMP_LEN = 32
CMP_STRIDE = 16
CMP_HIDDEN = 128
SLC_LEN = 64
SLC_TOPN = 16
WINDOW = 512
N_GROUPS = 4
EXPERTS_PER_GROUP = 8
N_EXPERTS = N_GROUPS * EXPERTS_PER_GROUP
D_EXPERT = 256
MOE_BLOCK = 256

IN_SPLITS = (A_HEADS * HEAD_DIM, HEAD_DIM, HEAD_DIM, IDX_HEADS * IDX_DIM, IDX_DIM, IDX_HEADS,
             B_HEADS * HEAD_DIM, 6 * B_KV_HEADS * HEAD_DIM, 3 * B_HEADS, D_MODEL, D_MODEL)
IN_OFFS = tuple(int(v) for v in np.cumsum((0,) + IN_SPLITS)[:-1])

LANES = 128
MASK_NEG = -3e30
M_INIT = -1e30
VMEM_LIMIT = 48 * 1024 * 1024
SCALE = HEAD_DIM ** -0.5 * float(np.log2(np.e))

F32 = jnp.float32
BF16 = jnp.bfloat16
I32 = jnp.int32


def _cparams(sem):
    return pltpu.CompilerParams(dimension_semantics=sem, vmem_limit_bytes=VMEM_LIMIT)


P_QA = 0
P_KVA = 512
P_QI = 640
P_KIW = 896
P_QB = 1024
P_KVB = 1536
P_CMP = 2048
P_GB = 2304
P_GATE = 2432
P_WIDTH = 4480


def _in_perm():
    idx = -np.ones((P_WIDTH,), np.int64)
    o_qa, o_ka, o_va, o_qi, o_ki, o_wi, o_qb, o_kvb, o_gb, o_ga, o_gbb = IN_OFFS
    idx[P_QA:P_QA + 512] = o_qa + np.arange(512)
    idx[P_KVA:P_KVA + 64] = o_ka + np.arange(64)
    idx[P_KVA + 64:P_KVA + 128] = o_va + np.arange(64)
    idx[P_QI:P_QI + 256] = o_qi + np.arange(256)
    idx[P_KIW:P_KIW + 32] = o_ki + np.arange(32)
    idx[P_KIW + 32:P_KIW + 40] = o_wi + np.arange(8)
    idx[P_QB:P_QB + 512] = o_qb + np.arange(512)

    def kvb_col(s, k):
        return o_kvb + s * (B_KV_HEADS * HEAD_DIM) + k * HEAD_DIM + np.arange(HEAD_DIM)

    for k in range(B_KV_HEADS):
        for j, s in enumerate((2, 3, 4, 5)):
            lo = P_KVB + k * 256 + j * 64
            idx[lo:lo + 64] = kvb_col(s, k)
    for j, (s, k) in enumerate(((0, 0), (0, 1), (1, 0), (1, 1))):
        lo = P_CMP + j * 64
        idx[lo:lo + 64] = kvb_col(s, k)
    idx[P_GB:P_GB + 24] = o_gb + np.arange(24)
    idx[P_GATE:P_GATE + 1024] = o_ga + np.arange(1024)
    idx[P_GATE + 1024:P_GATE + 2048] = o_gbb + np.arange(1024)
    return idx


_IN_PERM = _in_perm()


def _permute_columns(w):
    pieces, start = [], 0
    for pos in range(1, P_WIDTH + 1):
        if pos < P_WIDTH:
            prev, cur = int(_IN_PERM[pos - 1]), int(_IN_PERM[pos])
            same_run = (prev < 0 and cur < 0) or (prev >= 0 and cur == prev + 1)
        if pos == P_WIDTH or not same_run:
            src = int(_IN_PERM[start])
            pieces.append(jnp.zeros((w.shape[0], pos - start), w.dtype) if src < 0 else w[:, src:src + pos - start])
            start = pos
    return jnp.concatenate(pieces, axis=1)


def _trig_kernel(pos_ref, freq_ref, cos_ref, sin_ref):
    ang = pos_ref[...] * freq_ref[0]
    cos_ref[...] = jnp.cos(ang)
    sin_ref[...] = jnp.sin(ang)


def _rope_tables(positions):
    n = positions.size
    pos = positions.reshape(n).astype(F32)
    h64, h32 = HEAD_DIM // 2, IDX_DIM // 2
    f64 = ROPE_THETA ** (-jnp.arange(h64, dtype=F32) / h64)
    f32_ = ROPE_THETA ** (-jnp.arange(h32, dtype=F32) / h32)
    rows64, rows32 = n * h64 // LANES, n * h32 // LANES
    pos_rep = jnp.concatenate([jnp.repeat(pos, h64).reshape(rows64, LANES),
                               jnp.repeat(pos, h32).reshape(rows32, LANES)], axis=0)
    freq = jnp.stack([jnp.tile(f64, LANES // h64), jnp.tile(f32_, LANES // h32)])[:, None, :]
    rows = rows64 + rows32
    tr = rows32 if rows32 <= 1024 else 1024
    assert rows64 % tr == 0 and rows32 % tr == 0
    n64_tiles = rows64 // tr
    cos, sin = pl.pallas_call(
        _trig_kernel,
        grid=(rows // tr,),
        in_specs=[pl.BlockSpec((tr, LANES), lambda i: (i, 0)),
                  pl.BlockSpec((1, 1, LANES), lambda i: (jnp.where(i >= n64_tiles, 1, 0), 0, 0))],
        out_specs=[pl.BlockSpec((tr, LANES), lambda i: (i, 0))] * 2,
        out_shape=[jax.ShapeDtypeStruct((rows, LANES), F32)] * 2,
        compiler_params=_cparams(("parallel",)),
        name="trig",
    )(pos_rep, freq)
    c64 = cos[:rows64].reshape(n, h64)
    s64 = sin[:rows64].reshape(n, h64)
    c32 = cos[rows64:].reshape(n, h32)
    s32 = sin[rows64:].reshape(n, h32)
    cos64 = jnp.tile(c64, (1, LANES // h64))
    sin64 = jnp.tile(jnp.concatenate([-s64, s64], axis=1), (1, LANES // HEAD_DIM))
    cos32 = jnp.tile(c32, (1, LANES // h32))
    sin32 = jnp.tile(jnp.concatenate([-s32, s32], axis=1), (1, LANES // IDX_DIM))
    return cos64, sin64, cos32, sin32


def _rope_slab(x, cos, sin, half):
    lane = lax.broadcasted_iota(I32, x.shape, 1)
    first = (lane % (2 * half)) < half
    rot = jnp.where(first, pltpu.roll(x, LANES - half, 1), pltpu.roll(x, half, 1))
    return x * cos + rot * sin


def _inproj_kernel(x_ref, g_ref, w_ref, c64_ref, s64_ref, c32_ref, s32_ref,
                   qa_ref, kva_ref, qi_ref, kiw_ref, qbn_ref, qbr_ref, kvb_ref, cmp_ref, gsig_ref, gate_ref, *, seq):
    x = x_ref[...]
    ms = jnp.mean(x * x, axis=-1, keepdims=True)
    h = (x * lax.rsqrt(ms + RMS_EPS) * g_ref[...]).astype(BF16)
    c64, s64, c32, s32 = c64_ref[...], s64_ref[...], c32_ref[...], s32_ref[...]
    lane = lax.broadcasted_iota(I32, c64.shape, 1)

    def proj(lo, width):
        return jnp.dot(h, w_ref[:, lo:lo + width], preferred_element_type=F32)

    def slab(p, j):
        return p[:, j * LANES:(j + 1) * LANES]

    p = proj(P_QA, 512)
    for j in range(4):
        qa_ref[:, j * LANES:(j + 1) * LANES] = (_rope_slab(slab(p, j), c64, s64, 32) * SCALE).astype(BF16)
    p = proj(P_KVA, 128)
    kva_ref[...] = jnp.where(lane < HEAD_DIM, _rope_slab(p, c64, s64, 32), p).astype(BF16)
    p = proj(P_QI, 256)
    for j in range(2):
        qi_ref[:, j * LANES:(j + 1) * LANES] = _rope_slab(slab(p, j), c32, s32, 16).astype(BF16)
    p = proj(P_KIW, 128)
    kiw_ref[...] = jnp.where(lane < IDX_DIM, _rope_slab(p, c32, s32, 16), p)
    p = proj(P_QB, 512)
    qbn_ref[...] = (p * SCALE).astype(BF16)
    for j in range(4):
        qbr_ref[:, j * LANES:(j + 1) * LANES] = (_rope_slab(slab(p, j), c64, s64, 32) * SCALE).astype(BF16)
    p = proj(P_KVB, 512)
    tm = x.shape[0]
    pos = (pl.program_id(0) * tm) % seq + lax.broadcasted_iota(I32, c64.shape, 0)
    blk_hot = jnp.where(lane - HEAD_DIM == lax.shift_right_logical(pos, SLC_LEN.bit_length() - 1), 1.0, 0.0)
    for k in range(B_KV_HEADS):
        slc, win = slab(p, 2 * k), slab(p, 2 * k + 1)
        base = 3 * k * LANES
        kvb_ref[:, base:base + LANES] = jnp.where(lane < HEAD_DIM, _rope_slab(slc, c64, s64, 32), blk_hot).astype(BF16)
        kvb_ref[:, base + LANES:base + 2 * LANES] = jnp.where(lane < HEAD_DIM, _rope_slab(win, c64, s64, 32), win).astype(BF16)
        kvb_ref[:, base + 2 * LANES:base + 3 * LANES] = jnp.where(lane < HEAD_DIM, 0.0, slc).astype(BF16)
    cmp_ref[...] = proj(P_CMP, 256)
    gsig_ref[...] = jax.nn.sigmoid(proj(P_GB, 128))
    for j in range(2):
        gate_ref[:, j * 1024:(j + 1) * 1024] = jax.nn.sigmoid(proj(P_GATE + j * 1024, 1024)).astype(BF16)


def _inproj(x2, g, w_p, tables, tm, seq):
    n = x2.shape[0]
    assert seq % tm == 0
    row = lambda w: pl.BlockSpec((tm, w), lambda i: (i, 0))
    full = lambda a: pl.BlockSpec(a.shape, lambda i: (0, 0))
    outs = [(512, BF16), (128, BF16), (256, BF16), (128, F32), (512, BF16), (512, BF16), (3 * LANES * B_KV_HEADS, BF16),
            (256, F32), (128, F32), (2048, BF16)]
    return pl.pallas_call(
        functools.partial(_inproj_kernel, seq=seq),
        grid=(n // tm,),
        in_specs=[row(D_MODEL), full(g), full(w_p)] + [row(LANES)] * 4,
        out_specs=[row(w) for w, _ in outs],
        out_shape=[jax.ShapeDtypeStruct((n, w), dt) for w, dt in outs],
        compiler_params=_cparams(("parallel",)),
        name="inproj",
    )(x2, g, w_p, *tables)


def _compress_kernel(g_ref, pe_ref, w1_ref, w2_ref, o_ref):
    g = g_ref[0, 0]
    half = CMP_STRIDE * HEAD_DIM
    a = jnp.dot((g + pe_ref[0, :, :half]).astype(BF16), w1_ref[0, :half, :], preferred_element_type=F32)
    b = jnp.dot((g + pe_ref[0, :, half:]).astype(BF16), w1_ref[0, half:, :], preferred_element_type=F32)
    n_grp = g.shape[0]
    hid = a + pltpu.roll(b, n_grp - 1, 0)
    hid = hid * jax.nn.sigmoid(hid)
    o_ref[0, 0] = jnp.dot(hid.astype(BF16), w2_ref[0], preferred_element_type=F32).astype(BF16)


def _compress(cmp4, pe, w1, w2):
    bsz, _, n_grp, width = cmp4.shape
    return pl.pallas_call(
        _compress_kernel,
        grid=(bsz, 4),
        in_specs=[pl.BlockSpec((1, 1, n_grp, width), lambda b, a: (b, a, 0, 0)),
                  pl.BlockSpec((1, 1, 2 * width), lambda b, a: (a // 2, 0, 0)),
                  pl.BlockSpec((1, 2 * width, CMP_HIDDEN), lambda b, a: (a // 2, 0, 0)),
                  pl.BlockSpec((1, CMP_HIDDEN, HEAD_DIM), lambda b, a: (a // 2, 0, 0))],
        out_specs=pl.BlockSpec((1, 1, n_grp, HEAD_DIM), lambda b, a: (b, a, 0, 0)),
        out_shape=jax.ShapeDtypeStruct((bsz, 4, n_grp, HEAD_DIM), BF16),
        compiler_params=_cparams(("parallel", "parallel")),
        name="compress",
    )(cmp4, pe, w1, w2)


def _reset(m_ref, l_ref, acc_ref):
    m_ref[...] = jnp.full(m_ref.shape, M_INIT, F32)
    l_ref[...] = jnp.zeros(l_ref.shape, F32)
    acc_ref[...] = jnp.zeros(acc_ref.shape, F32)


def _stack_heads_t(q, n_heads):
    t = q.shape[0]
    lane = lax.broadcasted_iota(I32, (t, LANES), 1)
    parts = []
    for h in range(n_heads):
        slab = q[:, (h // 2) * LANES:(h // 2 + 1) * LANES].astype(F32)
        if h % 2:
            slab = pltpu.roll(slab, HEAD_DIM, 1)
        parts.append(jnp.where(lane < HEAD_DIM, slab, 0.0).T)
    return jnp.concatenate(parts, axis=1).astype(BF16)


def _unstack_heads_t(o_t, n_heads, t):
    lane = lax.broadcasted_iota(I32, (t, LANES), 1)
    slabs = []
    for j in range(n_heads // 2):
        even = pltpu.roll(o_t[:, (2 * j) * t:(2 * j + 1) * t].T, HEAD_DIM, 1)
        slabs.append(jnp.where(lane < HEAD_DIM, even, o_t[:, (2 * j + 1) * t:(2 * j + 2) * t].T))
    return jnp.concatenate(slabs, axis=1)


def _tn_dot(a, b):
    return lax.dot_general(a, b, (((0,), (0,)), ((), ())), preferred_element_type=F32)


def _online_update_t(k_c, q_t, bias_t, v_c, n_heads, t, m_ref, l_ref, acc_ref, heads_per_dot=None):
    HEADS_PER_DOT = heads_per_dot or n_heads
    for h0 in range(0, n_heads, HEADS_PER_DOT):
        grp = slice(h0 * t, (h0 + HEADS_PER_DOT) * t)
        s_g = jnp.dot(k_c, q_t[:, grp], preferred_element_type=F32)
        ps, alphas = [], []
        for h in range(HEADS_PER_DOT):
            cols = slice((h0 + h) * t, (h0 + h + 1) * t)
            sh = s_g[:, h * t:(h + 1) * t]
            if bias_t is not None:
                sh = sh + bias_t
            m_old = m_ref[:, cols]
            m_new = jnp.maximum(m_old, jnp.max(sh, axis=0, keepdims=True))
            alpha = jnp.exp2(m_old - m_new)
            p = jnp.exp2(sh - m_new)
            l_ref[:, cols] = alpha * l_ref[:, cols] + jnp.sum(p, axis=0, keepdims=True)
            m_ref[:, cols] = m_new
            ps.append(p.astype(BF16))
            alphas.append(alpha)
        acc_ref[:, grp] = (jnp.concatenate(alphas, axis=1) * acc_ref[:, grp]
                           + _tn_dot(v_c, jnp.concatenate(ps, axis=1)))


INT_MIN = -2 ** 31


def _chunk_loop(n_keys, sizes, body, init):
    big, small = sizes
    n_big = n_keys // big
    carry = lax.fori_loop(0, n_big, lambda c, x: body(pl.multiple_of(c * big, big), big, x), init)
    rest = n_big * big
    n_small = (n_keys - rest + small - 1) // small
    return lax.fori_loop(0, n_small, lambda c, x: body(pl.multiple_of(rest + c * small, small), small, x), carry)


def _dsa_kernel(qa_ref, qi_ref, wq_ref, kva_ref, kiw_ref, o_ref,
                key_scr, m_scr, l_scr, acc_scr, *, t, ck, k_sel):
    i = pl.program_id(1)
    t0 = i * t
    n_keys = t0 + t
    qpos = t0 + lax.broadcasted_iota(I32, (1, t), 1)

    def key_index(k0, size):
        return k0 + lax.broadcasted_iota(I32, (size, t), 0)

    qi_t = qi_ref[0].astype(F32).T
    w_t = wq_ref[0].T
    zpad = jnp.zeros((LANES - IDX_DIM, t), F32)
    qi_all = jnp.concatenate([jnp.concatenate([qi_t[h * IDX_DIM:(h + 1) * IDX_DIM], zpad], axis=0)
                              for h in range(IDX_HEADS)], axis=1).astype(BF16)

    def score_chunk(k0, size, carry):
        kic = kiw_ref[0, pl.ds(k0, size), :].astype(BF16)
        lg = jnp.dot(kic, qi_all, preferred_element_type=F32)
        acc = jnp.zeros((size, t), F32)
        for h in range(IDX_HEADS):
            acc = acc + w_t[IDX_DIM + h:IDX_DIM + h + 1] * jnp.maximum(lg[:, h * t:(h + 1) * t], 0.0)
        key_scr[pl.ds(k0, size), :] = jnp.where(key_index(k0, size) <= qpos, acc, -jnp.inf)
        return carry

    _chunk_loop(n_keys, ck, score_chunk, 0)

    n_part = 8

    def count_where(pred):
        def body(k0, size, cnt):
            hit = jnp.where(pred(key_scr[pl.ds(k0, size), :], k0, size), 1.0, 0.0)
            return cnt + jnp.sum(hit.reshape(n_part, size // (8 * n_part), 8, t), axis=1)
        cnt = _chunk_loop(n_keys, ck, body, jnp.zeros((n_part, 8, t), F32))
        return jnp.sum(jnp.sum(cnt, axis=0), axis=0, keepdims=True)

    def as_score(code):
        return pltpu.bitcast(jnp.where(code < 0, code ^ 0x7FFFFFFF, code), F32)

    def bit_step(b, lo):
        cand = lo + lax.shift_left(jnp.int32(1), 31 - b)
        v = as_score(cand)
        return jnp.where(count_where(lambda sc, k0, size: sc >= v) >= k_sel, cand, lo)

    code = lax.fori_loop(0, 32, bit_step, jnp.full((1, t), INT_MIN, I32))
    thr = jnp.where(code == INT_MIN, -jnp.inf, as_score(code))
    n_gt = count_where(lambda sc, k0, size: sc > thr)
    n_ge = count_where(lambda sc, k0, size: sc >= thr)
    need = k_sel - n_gt

    def tie_bound():
        n_bits = int(key_scr.shape[0]).bit_length()

        def step(b, j):
            cand = j + lax.shift_left(jnp.int32(1), n_bits - 1 - b)
            below = count_where(lambda sc, k0, size: (sc == thr) & (key_index(k0, size) < cand))
            return jnp.where(below <= need, cand, j)
        return lax.fori_loop(0, n_bits, step, jnp.zeros((1, t), I32))

    has_excess = jnp.max(jnp.where(n_ge > k_sel, 1.0, 0.0)) > 0.0
    jbound = lax.cond(has_excess, tie_bound, lambda: jnp.full((1, t), 2 ** 30, I32))

    q_t = _stack_heads_t(qa_ref[0], A_HEADS)
    _reset(m_scr, l_scr, acc_scr)

    def attn_chunk(k0, size, carry):
        key = key_scr[pl.ds(k0, size), :]
        idx = key_index(k0, size)
        sel = ((key > thr) | ((key == thr) & (idx < jbound))) & (idx <= qpos)
        bias_t = jnp.where(sel, 0.0, MASK_NEG)
        kv = kva_ref[0, pl.ds(k0, size), :]
        _online_update_t(kv, q_t, bias_t, kv, A_HEADS, t, m_scr, l_scr, acc_scr)
        return carry

    _chunk_loop(n_keys, ck, attn_chunk, 0)
    o_t = acc_scr[...] / l_scr[...]
    o_ref[0] = _unstack_heads_t(o_t, A_HEADS, t).astype(BF16)


def _dsa(qa, qi, kiw, kva, t, ck):
    bsz, seq, _ = qa.shape
    k_sel = min(DSA_TOPK_MAX, seq // 4)
    kern = functools.partial(_dsa_kernel, t=t, ck=ck, k_sel=k_sel)
    blk = lambda w: pl.BlockSpec((1, t, w), lambda b, i: (b, i, 0))
    res = lambda w: pl.BlockSpec((1, seq, w), lambda b, i: (b, 0, 0))
    cols = A_HEADS * t
    return pl.pallas_call(
        kern,
        grid=(bsz, seq // t),
        in_specs=[blk(512), blk(256), blk(LANES), res(LANES), res(LANES)],
        out_specs=blk(512),
        out_shape=jax.ShapeDtypeStruct((bsz, seq, 512), BF16),
        scratch_shapes=[pltpu.VMEM((seq, t), F32), pltpu.VMEM((1, cols), F32), pltpu.VMEM((1, cols), F32),
                        pltpu.VMEM((LANES, cols), F32)],
        compiler_params=_cparams(("parallel", "arbitrary")),
        name="dsa",
    )(qa, qi, kiw, kva, kiw)


def _nsa_kernel(qn_ref, qr_ref, g_ref, kcv_ref, kv_ref, c2s_ref, o_ref,
                m_scr, l_scr, acc_scr, *, t, ck, n_slc, n_sel):
    kh = pl.program_id(1)
    i = pl.program_id(2)
    t0 = i * t
    qpos = t0 + lax.broadcasted_iota(I32, (1, t), 1)
    qn_t = _stack_heads_t(qn_ref[0], B_GROUP)
    qr_t = _stack_heads_t(qr_ref[0], B_GROUP)
    heads = [slice(g * t, (g + 1) * t) for g in range(B_GROUP)]

    kcv = kcv_ref[0, 0]
    n_cp = kcv.shape[0]
    s_t = jnp.dot(kcv, qn_t, preferred_element_type=F32)
    cmp_end = lax.broadcasted_iota(I32, (n_cp, t), 0) * CMP_STRIDE + (CMP_LEN - 1)
    vis = cmp_end <= qpos
    probs = []
    for cols in heads:
        sh = jnp.where(vis, s_t[:, cols], M_INIT)
        p = jnp.where(vis, jnp.exp2(sh - jnp.max(sh, axis=0, keepdims=True)), 0.0)
        den = jnp.sum(p, axis=0, keepdims=True)
        probs.append(p * jnp.where(den > 0.0, 1.0 / den, 0.0))
    o_cmp_t = _tn_dot(kcv, jnp.concatenate(probs, axis=1).astype(BF16))

    psum = probs[0]
    for g in range(1, B_GROUP):
        psum = psum + probs[g]
    p_hi = psum.astype(BF16)
    p_lo = (psum - p_hi.astype(F32)).astype(BF16)
    imp = (jnp.dot(c2s_ref[...], p_hi, preferred_element_type=F32)
           + jnp.dot(c2s_ref[...], p_lo, preferred_element_type=F32))
    jx = lax.broadcasted_iota(I32, (n_slc, t), 0)
    admiss = jx * SLC_LEN <= qpos
    cur = lax.shift_right_logical(qpos, SLC_LEN.bit_length() - 1)
    forced = admiss & ((jx == 0) | (jx == cur) | (jx == cur - 1))
    score = jnp.where(forced, jnp.inf, jnp.where(admiss, imp, -jnp.inf))
    rank = jnp.zeros((n_slc, t), F32)
    for j2 in range(n_slc):
        row = score[j2:j2 + 1, :]
        ahead = (row > score) | ((row == score) & (j2 < jx))
        rank = rank + jnp.where(ahead, 1.0, 0.0)
    selbias = jnp.where(rank < n_sel, 0.0, MASK_NEG)
    if n_slc < HEAD_DIM:
        selbias = jnp.concatenate([selbias, jnp.zeros((HEAD_DIM - n_slc, t), F32)], axis=0)
    q_aug_t = jnp.concatenate([qr_t[:HEAD_DIM], jnp.concatenate([selbias.astype(BF16)] * B_GROUP, axis=1)], axis=0)

    _reset(m_scr, l_scr, acc_scr)
    def slc_chunk(k0, size, causal):
        bias_t = None
        if causal:
            bias_t = jnp.where(k0 + lax.broadcasted_iota(I32, (size, t), 0) <= qpos, 0.0, MASK_NEG)
        _online_update_t(kv_ref[0, pl.ds(k0, size), :LANES], q_aug_t, bias_t, kv_ref[0, pl.ds(k0, size), 2 * LANES:],
                         B_GROUP, t, m_scr, l_scr, acc_scr)
        return 0

    diag = ck[1]
    k_diag = pl.multiple_of(t0 // diag * diag, diag)
    _chunk_loop(k_diag, ck, lambda k0, size, carry: slc_chunk(k0, size, False), 0)
    slc_chunk(k_diag, diag, True)
    o_slc_t = acc_scr[...] / l_scr[...]

    ww = WINDOW + t
    w0 = pl.multiple_of(jnp.maximum(t0 - WINDOW, 0), t)
    kvw = kv_ref[0, pl.ds(w0, ww), LANES:2 * LANES]
    s_w = jnp.dot(kvw, qr_t, preferred_element_type=F32)
    rel = qpos - (w0 + lax.broadcasted_iota(I32, (ww, t), 0))
    bias_w = jnp.where((rel >= 0) & (rel < WINDOW), 0.0, MASK_NEG)
    pws, dens = [], []
    for cols in heads:
        sh = s_w[:, cols] + bias_w
        p = jnp.exp2(sh - jnp.max(sh, axis=0, keepdims=True))
        dens.append(jnp.sum(p, axis=0, keepdims=True))
        pws.append(p.astype(BF16))
    o_win_t = _tn_dot(kvw, jnp.concatenate(pws, axis=1)) / jnp.concatenate(dens, axis=1)

    g_t = g_ref[0].T
    grow = lax.broadcasted_iota(I32, g_t.shape, 0)
    outs = []
    for g, cols in enumerate(heads):
        hd = kh * B_GROUP + g

        def gate(branch):
            return jnp.sum(jnp.where(grow == branch * B_HEADS + hd, g_t, 0.0), axis=0, keepdims=True)

        outs.append(gate(0) * o_cmp_t[:, cols] + gate(1) * o_slc_t[:, cols] + gate(2) * o_win_t[:, cols])
    o_ref[0] = _unstack_heads_t(jnp.concatenate(outs, axis=1), B_GROUP, t).astype(BF16)


def _nsa(qbn, qbr, gsig, kcv, kvb, t, ck):
    bsz, seq, _ = qbn.shape
    n_cp = kcv.shape[2]
    n_slc = seq // SLC_LEN
    n_sel = min(SLC_TOPN, n_slc)
    assert n_slc <= HEAD_DIM and seq >= WINDOW + t and ck[1] % t == 0 and ck[0] % ck[1] == 0
    n = np.arange(n_cp)
    js = np.arange(n_slc) * SLC_LEN
    overlap = (n[None, :] * CMP_STRIDE < js[:, None] + SLC_LEN) & (n[None, :] * CMP_STRIDE + CMP_LEN > js[:, None])
    overlap &= (n[None, :] < n_cp - 1)
    c2s_t = jnp.asarray(overlap.astype(np.float32), BF16)
    kern = functools.partial(_nsa_kernel, t=t, ck=ck, n_slc=n_slc, n_sel=n_sel)
    cols = B_GROUP * t
    width = B_GROUP * HEAD_DIM
    qblk = pl.BlockSpec((1, t, width), lambda b, k, i: (b, i, k))
    return pl.pallas_call(
        kern,
        grid=(bsz, B_KV_HEADS, seq // t),
        in_specs=[qblk, qblk,
                  pl.BlockSpec((1, t, LANES), lambda b, k, i: (b, i, 0)),
                  pl.BlockSpec((1, 1, n_cp, LANES), lambda b, k, i: (b, k, 0, 0)),
                  pl.BlockSpec((1, seq, 3 * LANES), lambda b, k, i: (b, 0, k)),
                  pl.BlockSpec((n_slc, n_cp), lambda b, k, i: (0, 0))],
        out_specs=qblk,
        out_shape=jax.ShapeDtypeStruct((bsz, seq, B_HEADS * HEAD_DIM), BF16),
        scratch_shapes=[pltpu.VMEM((1, cols), F32), pltpu.VMEM((1, cols), F32), pltpu.VMEM((LANES, cols), F32)],
        compiler_params=_cparams(("parallel", "parallel", "arbitrary")),
        name="nsa",
    )(qbn, qbr, gsig, kcv, kvb, c2s_t)


R_EXP = 0
R_GRP = N_EXPERTS


def _merge_kernel(oa_ref, ob_ref, gate_ref, x_ref, wa_ref, wb_ref, wo_ref, g_ref, wr_ref, br_ref,
                  x1_ref, h2_ref, route_ref):
    ya = jnp.dot(oa_ref[...], wa_ref[...], preferred_element_type=F32)
    yb = jnp.dot(ob_ref[...], wb_ref[...], preferred_element_type=F32)
    merged = gate_ref[:, :D_MODEL].astype(F32) * ya + gate_ref[:, D_MODEL:].astype(F32) * yb
    x1 = x_ref[...] + jnp.dot(merged.astype(BF16), wo_ref[...], preferred_element_type=F32)
    x1_ref[...] = x1
    ms = jnp.mean(x1 * x1, axis=-1, keepdims=True)
    h2 = x1 * lax.rsqrt(ms + RMS_EPS) * g_ref[...]
    h2_ref[...] = h2

    h_hi = h2.astype(BF16)
    h_lo = (h2 - h_hi.astype(F32)).astype(BF16)
    hi_both = jnp.dot(h_hi, wr_ref[...], preferred_element_type=F32)
    logit = (hi_both[:, :LANES] + hi_both[:, LANES:]
             + jnp.dot(h_lo, wr_ref[:, :LANES], preferred_element_type=F32) + br_ref[...])
    lane = lax.broadcasted_iota(I32, logit.shape, 1)
    lane_f = lane.astype(F32)

    def first_argmax(v):
        mx = jnp.max(v, axis=-1, keepdims=True)
        return mx, jnp.min(jnp.where(v == mx, lane_f, float(LANES)), axis=-1, keepdims=True)

    is_grp = (lane >= R_GRP) & (lane < R_GRP + N_GROUPS)
    gl = jnp.where(is_grp, logit, -jnp.inf)
    gmax, garg = first_argmax(gl)
    g_sel = (garg - R_GRP).astype(I32)
    g_prob = 1.0 / jnp.sum(jnp.where(is_grp, jnp.exp(gl - gmax), 0.0), axis=-1, keepdims=True)
    grp_of_lane = lax.shift_right_logical(lane, EXPERTS_PER_GROUP.bit_length() - 1)
    in_grp = (lane < N_EXPERTS) & (grp_of_lane == g_sel)
    el = jnp.where(in_grp, logit, -jnp.inf)
    v1, i1 = first_argmax(el)
    v2, i2 = first_argmax(jnp.where(lane_f == i1, -jnp.inf, el))
    e2 = jnp.exp(v2 - v1)
    w1 = g_prob / (1.0 + e2)
    w2 = g_prob * e2 / (1.0 + e2)
    route_ref[...] = jnp.where(lane == 0, i1, jnp.where(lane == 1, i2,
                               jnp.where(lane == 2, w1, jnp.where(lane == 3, w2, 0.0))))


def _merge(o_a, o_b, gates, x2, wa, wb, wo, g, wr, br, tm):
    n = x2.shape[0]
    row = lambda w: pl.BlockSpec((tm, w), lambda i: (i, 0))
    full = lambda a: pl.BlockSpec(a.shape, lambda i: (0, 0))
    return pl.pallas_call(
        _merge_kernel,
        grid=(n // tm,),
        in_specs=[row(512), row(512), row(2048), row(D_MODEL), full(wa), full(wb), full(wo), full(g), full(wr), full(br)],
        out_specs=[row(D_MODEL), row(D_MODEL), row(LANES)],
        out_shape=[jax.ShapeDtypeStruct((n, D_MODEL), F32), jax.ShapeDtypeStruct((n, D_MODEL), F32),
                   jax.ShapeDtypeStruct((n, LANES), F32)],
        compiler_params=_cparams(("parallel",)),
        name="merge",
    )(o_a, o_b, gates, x2, wa, wb, wo, g, wr, br)


def _expert_kernel(be_ref, slot_ref, h_hbm, wgu_ref, wd_ref, y_ref, buf, sem):
    i = pl.program_id(0)

    def gather(block, slot):
        base = block * MOE_BLOCK

        def issue(r, carry):
            pltpu.make_async_copy(h_hbm.at[pl.ds(slot_ref[base + r], 1)], buf.at[slot, pl.ds(r, 1)],
                                  sem.at[slot]).start()
            return carry

        lax.fori_loop(0, MOE_BLOCK, issue, 0, unroll=8)

    @pl.when(i == 0)
    def _():
        gather(0, 0)

    @pl.when(i + 1 < pl.num_programs(0))
    def _():
        gather(i + 1, (i + 1) % 2)

    slot = i % 2
    pltpu.make_async_copy(h_hbm.at[pl.ds(0, MOE_BLOCK)], buf.at[slot], sem.at[slot]).wait()
    xb = buf[slot].astype(BF16)
    gu = jnp.dot(xb, wgu_ref[0], preferred_element_type=F32)
    gate, up = gu[:, :D_EXPERT], gu[:, D_EXPERT:]
    act = gate * jax.nn.sigmoid(gate) * up
    y_ref[...] = jnp.dot(act.astype(BF16), wd_ref[0], preferred_element_type=F32)


def _experts(blk_expert, slot_tok, h2, wgu, wd):
    n_blocks = blk_expert.shape[0]
    grid_spec = pltpu.PrefetchScalarGridSpec(
        num_scalar_prefetch=2,
        grid=(n_blocks,),
        in_specs=[pl.BlockSpec(memory_space=pl.ANY),
                  pl.BlockSpec((1, D_MODEL, 2 * D_EXPERT), lambda i, be, st: (be[i], 0, 0)),
                  pl.BlockSpec((1, D_EXPERT, D_MODEL), lambda i, be, st: (be[i], 0, 0))],
        out_specs=pl.BlockSpec((MOE_BLOCK, D_MODEL), lambda i, be, st: (i, 0)),
        scratch_shapes=[pltpu.VMEM((2, MOE_BLOCK, D_MODEL), F32), pltpu.SemaphoreType.DMA((2,))],
    )
    return pl.pallas_call(
        _expert_kernel,
        grid_spec=grid_spec,
        out_shape=jax.ShapeDtypeStruct((n_blocks * MOE_BLOCK, D_MODEL), F32),
        compiler_params=_cparams(("arbitrary",)),
        name="experts",
    )(blk_expert, slot_tok, h2, wgu, wd)


def _combine_kernel(dest_ref, y_hbm, x1_ref, route_ref, g_ref, o_ref, buf, sem, *, tm):
    i = pl.program_id(0)

    def gather(tile, slot):
        base = tile * tm * 2

        def issue(r, carry):
            for k in range(2):
                pltpu.make_async_copy(y_hbm.at[pl.ds(dest_ref[base + 2 * r + k], 1)],
                                      buf.at[slot, pl.ds(k * tm + r, 1)], sem.at[slot]).start()
            return carry

        lax.fori_loop(0, tm, issue, 0, unroll=4)

    @pl.when(i == 0)
    def _():
        gather(0, 0)

    @pl.when(i + 1 < pl.num_programs(0))
    def _():
        gather(i + 1, (i + 1) % 2)

    slot = i % 2
    pltpu.make_async_copy(y_hbm.at[pl.ds(0, 2 * tm)], buf.at[slot], sem.at[slot]).wait()
    route = route_ref[...]
    x = x1_ref[...] + route[:, 2:3] * buf[slot, :tm] + route[:, 3:4] * buf[slot, tm:]
    ms = jnp.mean(x * x, axis=-1, keepdims=True)
    o_ref[...] = x * lax.rsqrt(ms + RMS_EPS) * g_ref[...]


def _combine(dest, y_buf, x1, route, g, tm):
    n = x1.shape[0]
    grid_spec = pltpu.PrefetchScalarGridSpec(
        num_scalar_prefetch=1,
        grid=(n // tm,),
        in_specs=[pl.BlockSpec(memory_space=pl.ANY),
                  pl.BlockSpec((tm, D_MODEL), lambda i, d: (i, 0)),
                  pl.BlockSpec((tm, LANES), lambda i, d: (i, 0)),
                  pl.BlockSpec((1, D_MODEL), lambda i, d: (0, 0))],
        out_specs=pl.BlockSpec((tm, D_MODEL), lambda i, d: (i, 0)),
        scratch_shapes=[pltpu.VMEM((2, 2 * tm, D_MODEL), F32), pltpu.SemaphoreType.DMA((2,))],
    )
    return pl.pallas_call(
        functools.partial(_combine_kernel, tm=tm),
        grid_spec=grid_spec,
        out_shape=jax.ShapeDtypeStruct((n, D_MODEL), F32),
        compiler_params=_cparams(("arbitrary",)),
        name="combine",
    )(dest, y_buf, x1, route, g)


def _dispatch_plan(expert):
    n_tok = expert.shape[0]
    n_asg = n_tok * 2
    e_flat = expert.reshape(n_asg)
    onehot = (e_flat[:, None] == jnp.arange(N_EXPERTS, dtype=I32)[None, :]).astype(I32)
    csum = jnp.cumsum(onehot, axis=0)
    rank = jnp.take_along_axis(csum, e_flat[:, None], axis=1)[:, 0] - 1
    counts = csum[-1]
    padded = (counts + MOE_BLOCK - 1) // MOE_BLOCK * MOE_BLOCK
    pad_end = jnp.cumsum(padded)
    pad_start = pad_end - padded
    dest = (pad_start[e_flat] + rank).astype(I32)
    n_blocks = -(-n_asg // MOE_BLOCK) + N_EXPERTS
    slot_tok = jnp.zeros((n_blocks * MOE_BLOCK,), I32).at[dest].set(jnp.arange(n_asg, dtype=I32) // 2)
    blk_start = jnp.arange(n_blocks, dtype=I32) * MOE_BLOCK
    blk_expert = jnp.minimum(jnp.sum((pad_end[None, :] <= blk_start[:, None]).astype(I32), axis=1), N_EXPERTS - 1)
    return dest, slot_tok, blk_expert


def _pick(n, prefs):
    for p in prefs:
        if n % p == 0:
            return p
    return n


def _layer(x, positions, norm_mix, w_in, pe_k, w1_k, w2_k, pe_v, w1_v, w2_v, w_br_a, w_br_b, w_out,
           norm_ffn, w_group, b_group, w_expert, b_expert, w_gate_up, w_down, out_gain):
    bsz, seq, d = x.shape
    n = bsz * seq
    x2 = x.reshape(n, d)
    tables = _rope_tables(positions)

    w_p = _permute_columns(w_in.astype(BF16))
    tm = _pick(n, (512, 256, 128))
    qa, kva, qi, kiw, qbn, qbr, kvb, cmp_, gsig, gates = _inproj(x2, norm_mix[None, :], w_p, tables, tm, seq)

    n_grp = seq // CMP_STRIDE
    cmp4 = cmp_.reshape(bsz, seq, 4, HEAD_DIM).transpose(0, 2, 1, 3).reshape(bsz, 4, n_grp, CMP_STRIDE * HEAD_DIM)
    pe = jnp.stack([pe_k.reshape(1, -1), pe_v.reshape(1, -1)])
    kcvc = _compress(cmp4, pe, jnp.stack([w1_k, w1_v]).astype(BF16), jnp.stack([w2_k, w2_v]).astype(BF16))
    kcv = jnp.concatenate([kcvc[:, :B_KV_HEADS], kcvc[:, B_KV_HEADS:]], axis=-1)

    r3 = lambda a: a.reshape(bsz, seq, a.shape[-1])
    t = _pick(seq, (128,))
    ck = (_pick(seq, (1024, 512)), _pick(seq, (512,)))
    o_a = _dsa(r3(qa), r3(qi), r3(kiw), r3(kva), t, ck)
    o_b = _nsa(r3(qbn), r3(qbr), r3(gsig), kcv, r3(kvb), t, ck)

    wr = jnp.concatenate([w_expert, w_group, jnp.zeros((d, LANES - N_EXPERTS - N_GROUPS), F32)], axis=1)
    wr_hi = wr.astype(BF16)
    wr = jnp.concatenate([wr_hi, (wr - wr_hi.astype(F32)).astype(BF16)], axis=1)
    br = jnp.zeros((1, LANES), F32).at[0, R_EXP:R_EXP + N_EXPERTS].set(b_expert).at[0, R_GRP:R_GRP + N_GROUPS].set(b_group)
    x1, h2, route = _merge(o_a.reshape(n, 512), o_b.reshape(n, 512), gates, x2,
                           w_br_a.astype(BF16), w_br_b.astype(BF16), w_out.astype(BF16),
                           norm_ffn[None, :], wr, br, _pick(n, (512, 256, 128)))

    expert = route[:, :2].astype(I32)
    dest, slot_tok, blk_expert = _dispatch_plan(expert)
    y_buf = _experts(blk_expert, slot_tok, h2, w_gate_up.astype(BF16), w_down.astype(BF16))
    out = _combine(dest, y_buf, x1, route, out_gain[None, :], _pick(n, (256, 128)))
    return out.reshape(bsz, seq, d)


def kernel(x, positions, norm_mix, w_in, pe_k, w1_k, w2_k, pe_v, w1_v, w2_v, w_br_a, w_br_b, w_out, norm_ffn,
           w_group, b_group, w_expert, b_expert, w_gate_up, w_down, norm_final):
    assert norm_mix.shape[0] == 1, "single-layer block"
    l = 0
    return _layer(x, positions, norm_mix[l], w_in[l], pe_k[l], w1_k[l], w2_k[l], pe_v[l], w1_v[l], w2_v[l],
                  w_br_a[l], w_br_b[l], w_out[l], norm_ffn[l], w_group[l], b_group[l], w_expert[l], b_expert[l],
                  w_gate_up[l], w_down[l], norm_final)
```

```python
import functools

import numpy as np
import jax
import jax.numpy as jnp
from jax import lax
from jax.experimental import pallas as pl
from jax.experimental.pallas import tpu as pltpu

D_MODEL = 1024
HEAD_DIM = 64
ROPE_THETA = 10000.0
RMS_EPS = 1e-6
A_HEADS = 8
IDX_HEADS = 8
IDX_DIM = 32
DSA_TOPK_MAX = 256
B_HEADS = 8
B_KV_HEADS = 2
B_GROUP = B_HEADS // B_KV_HEADS
CMP_LEN = 32
CMP_STRIDE = 16
CMP_HIDDEN = 128
SLC_LEN = 64
SLC_TOPN = 16
WINDOW = 512
N_GROUPS = 4
EXPERTS_PER_GROUP = 8
N_EXPERTS = N_GROUPS * EXPERTS_PER_GROUP
D_EXPERT = 256
MOE_BLOCK = 256

IN_SPLITS = (A_HEADS * HEAD_DIM, HEAD_DIM, HEAD_DIM, IDX_HEADS * IDX_DIM, IDX_DIM, IDX_HEADS,
             B_HEADS * HEAD_DIM, 6 * B_KV_HEADS * HEAD_DIM, 3 * B_HEADS, D_MODEL, D_MODEL)
IN_OFFS = tuple(int(v) for v in np.cumsum((0,) + IN_SPLITS)[:-1])

LANES = 128
MASK_NEG = -3e30
M_INIT = -1e30
VMEM_LIMIT = 48 * 1024 * 1024
SCALE = HEAD_DIM ** -0.5 * float(np.log2(np.e))

F32 = jnp.float32
BF16 = jnp.bfloat16
I32 = jnp.int32


def _cparams(sem):
    return pltpu.CompilerParams(dimension_semantics=sem, vmem_limit_bytes=VMEM_LIMIT)


P_QA = 0
P_KVA = 512
P_QI = 640
P_KIW = 896
P_QB = 1024
P_KVB = 1536
P_CMP = 2048
P_GB = 2304
P_GATE = 2432
P_WIDTH = 4480


def _in_perm():
    idx = -np.ones((P_WIDTH,), np.int64)
    o_qa, o_ka, o_va, o_qi, o_ki, o_wi, o_qb, o_kvb, o_gb, o_ga, o_gbb = IN_OFFS
    idx[P_QA:P_QA + 512] = o_qa + np.arange(512)
    idx[P_KVA:P_KVA + 64] = o_ka + np.arange(64)
    idx[P_KVA + 64:P_KVA + 128] = o_va + np.arange(64)
    idx[P_QI:P_QI + 256] = o_qi + np.arange(256)
    idx[P_KIW:P_KIW + 32] = o_ki + np.arange(32)
    idx[P_KIW + 32:P_KIW + 40] = o_wi + np.arange(8)
    idx[P_QB:P_QB + 512] = o_qb + np.arange(512)

    def kvb_col(s, k):
        return o_kvb + s * (B_KV_HEADS * HEAD_DIM) + k * HEAD_DIM + np.arange(HEAD_DIM)

    for k in range(B_KV_HEADS):
        for j, s in enumerate((2, 3, 4, 5)):
            lo = P_KVB + k * 256 + j * 64
            idx[lo:lo + 64] = kvb_col(s, k)
    for j, (s, k) in enumerate(((0, 0), (0, 1), (1, 0), (1, 1))):
        lo = P_CMP + j * 64
        idx[lo:lo + 64] = kvb_col(s, k)
    idx[P_GB:P_GB + 24] = o_gb + np.arange(24)
    idx[P_GATE:P_GATE + 1024] = o_ga + np.arange(1024)
    idx[P_GATE + 1024:P_GATE + 2048] = o_gbb + np.arange(1024)
    return idx


_IN_PERM = _in_perm()


def _permute_columns(w):
    pieces, start = [], 0
    for pos in range(1, P_WIDTH + 1):
        if pos < P_WIDTH:
            prev, cur = int(_IN_PERM[pos - 1]), int(_IN_PERM[pos])
            same_run = (prev < 0 and cur < 0) or (prev >= 0 and cur == prev + 1)
        if pos == P_WIDTH or not same_run:
            src = int(_IN_PERM[start])
            pieces.append(jnp.zeros((w.shape[0], pos - start), w.dtype) if src < 0 else w[:, src:src + pos - start])
            start = pos
    return jnp.concatenate(pieces, axis=1)


def _trig_kernel(pos_ref, freq_ref, cos_ref, sin_ref):
    ang = pos_ref[...] * freq_ref[0]
    cos_ref[...] = jnp.cos(ang)
    sin_ref[...] = jnp.sin(ang)


def _rope_tables(positions):
    n = positions.size
    pos = positions.reshape(n).astype(F32)
    h64, h32 = HEAD_DIM // 2, IDX_DIM // 2
    f64 = ROPE_THETA ** (-jnp.arange(h64, dtype=F32) / h64)
    f32_ = ROPE_THETA ** (-jnp.arange(h32, dtype=F32) / h32)
    rows64, rows32 = n * h64 // LANES, n * h32 // LANES
    pos_rep = jnp.concatenate([jnp.repeat(pos, h64).reshape(rows64, LANES),
                               jnp.repeat(pos, h32).reshape(rows32, LANES)], axis=0)
    freq = jnp.stack([jnp.tile(f64, LANES // h64), jnp.tile(f32_, LANES // h32)])[:, None, :]
    rows = rows64 + rows32
    tr = rows32 if rows32 <= 1024 else 1024
    assert rows64 % tr == 0 and rows32 % tr == 0
    n64_tiles = rows64 // tr
    cos, sin = pl.pallas_call(
        _trig_kernel,
        grid=(rows // tr,),
        in_specs=[pl.BlockSpec((tr, LANES), lambda i: (i, 0)),
                  pl.BlockSpec((1, 1, LANES), lambda i: (jnp.where(i >= n64_tiles, 1, 0), 0, 0))],
        out_specs=[pl.BlockSpec((tr, LANES), lambda i: (i, 0))] * 2,
        out_shape=[jax.ShapeDtypeStruct((rows, LANES), F32)] * 2,
        compiler_params=_cparams(("parallel",)),
        name="trig",
    )(pos_rep, freq)
    c64 = cos[:rows64].reshape(n, h64)
    s64 = sin[:rows64].reshape(n, h64)
    c32 = cos[rows64:].reshape(n, h32)
    s32 = sin[rows64:].reshape(n, h32)
    cos64 = jnp.tile(c64, (1, LANES // h64))
    sin64 = jnp.tile(jnp.concatenate([-s64, s64], axis=1), (1, LANES // HEAD_DIM))
    cos32 = jnp.tile(c32, (1, LANES // h32))
    sin32 = jnp.tile(jnp.concatenate([-s32, s32], axis=1), (1, LANES // IDX_DIM))
    return cos64, sin64, cos32, sin32


def _rope_slab(x, cos, sin, half):
    lane = lax.broadcasted_iota(I32, x.shape, 1)
    first = (lane % (2 * half)) < half
    rot = jnp.where(first, pltpu.roll(x, LANES - half, 1), pltpu.roll(x, half, 1))
    return x * cos + rot * sin


def _inproj_kernel(x_ref, g_ref, w_ref, c64_ref, s64_ref, c32_ref, s32_ref,
                   qa_ref, kva_ref, qi_ref, kiw_ref, qbn_ref, qbr_ref, kvb_ref, cmp_ref, gsig_ref, gate_ref, *, seq):
    x = x_ref[...]
    ms = jnp.mean(x * x, axis=-1, keepdims=True)
    h = (x * lax.rsqrt(ms + RMS_EPS) * g_ref[...]).astype(BF16)
    c64, s64, c32, s32 = c64_ref[...], s64_ref[...], c32_ref[...], s32_ref[...]
    lane = lax.broadcasted_iota(I32, c64.shape, 1)

    def proj(lo, width):
        return jnp.dot(h, w_ref[:, lo:lo + width], preferred_element_type=F32)

    def slab(p, j):
        return p[:, j * LANES:(j + 1) * LANES]

    p = proj(P_QA, 512)
    for j in range(4):
        qa_ref[:, j * LANES:(j + 1) * LANES] = (_rope_slab(slab(p, j), c64, s64, 32) * SCALE).astype(BF16)
    p = proj(P_KVA, 128)
    kva_ref[:, :LANES] = jnp.where(lane < HEAD_DIM, _rope_slab(p, c64, s64, 32), p).astype(BF16)
    kva_ref[:, LANES:] = jnp.where(lane == 0, 1.0, jnp.where(lane < HEAD_DIM, 0.0, p)).astype(BF16)
    p = proj(P_QI, 256)
    for j in range(2):
        qi_ref[:, j * LANES:(j + 1) * LANES] = _rope_slab(slab(p, j), c32, s32, 16).astype(BF16)
    p = proj(P_KIW, 128)
    kiw_ref[...] = jnp.where(lane < IDX_DIM, _rope_slab(p, c32, s32, 16), p)
    p = proj(P_QB, 512)
    qbn_ref[...] = (p * SCALE).astype(BF16)
    for j in range(4):
        qbr_ref[:, j * LANES:(j + 1) * LANES] = (_rope_slab(slab(p, j), c64, s64, 32) * SCALE).astype(BF16)
    p = proj(P_KVB, 512)
    tm = x.shape[0]
    pos = (pl.program_id(0) * tm) % seq + lax.broadcasted_iota(I32, c64.shape, 0)
    blk_hot = jnp.where(lane - HEAD_DIM == lax.shift_right_logical(pos, SLC_LEN.bit_length() - 1), 1.0, 0.0)
    for k in range(B_KV_HEADS):
        slc, win = slab(p, 2 * k), slab(p, 2 * k + 1)
        base = 3 * k * LANES
        kvb_ref[:, base:base + LANES] = jnp.where(lane < HEAD_DIM, _rope_slab(slc, c64, s64, 32), blk_hot).astype(BF16)
        kvb_ref[:, base + LANES:base + 2 * LANES] = jnp.where(lane < HEAD_DIM, _rope_slab(win, c64, s64, 32), win).astype(BF16)
        kvb_ref[:, base + 2 * LANES:base + 3 * LANES] = jnp.where(
            lane == 0, 1.0, jnp.where(lane < HEAD_DIM, 0.0, slc)).astype(BF16)
    cmp_ref[...] = proj(P_CMP, 256)
    gsig_ref[...] = jax.nn.sigmoid(proj(P_GB, 128))
    for j in range(2):
        gate_ref[:, j * 1024:(j + 1) * 1024] = jax.nn.sigmoid(proj(P_GATE + j * 1024, 1024)).astype(BF16)


def _inproj(x2, g, w_p, tables, tm, seq):
    n = x2.shape[0]
    assert seq % tm == 0
    row = lambda w: pl.BlockSpec((tm, w), lambda i: (i, 0))
    full = lambda a: pl.BlockSpec(a.shape, lambda i: (0, 0))
    outs = [(512, BF16), (256, BF16), (256, BF16), (128, F32), (512, BF16), (512, BF16), (3 * LANES * B_KV_HEADS, BF16),
            (256, F32), (128, F32), (2048, BF16)]
    return pl.pallas_call(
        functools.partial(_inproj_kernel, seq=seq),
        grid=(n // tm,),
        in_specs=[row(D_MODEL), full(g), full(w_p)] + [row(LANES)] * 4,
        out_specs=[row(w) for w, _ in outs],
        out_shape=[jax.ShapeDtypeStruct((n, w), dt) for w, dt in outs],
        compiler_params=_cparams(("parallel",)),
        name="inproj",
    )(x2, g, w_p, *tables)


def _compress_kernel(g_ref, pe_ref, w1_ref, w2_ref, o_ref):
    g = g_ref[0, 0]
    half = CMP_STRIDE * HEAD_DIM
    a = jnp.dot((g + pe_ref[0, :, :half]).astype(BF16), w1_ref[0, :half, :], preferred_element_type=F32)
    b = jnp.dot((g + pe_ref[0, :, half:]).astype(BF16), w1_ref[0, half:, :], preferred_element_type=F32)
    n_grp = g.shape[0]
    hid = a + pltpu.roll(b, n_grp - 1, 0)
    hid = hid * jax.nn.sigmoid(hid)
    o_ref[0, 0] = jnp.dot(hid.astype(BF16), w2_ref[0], preferred_element_type=F32).astype(BF16)


def _compress(cmp4, pe, w1, w2):
    bsz, _, n_grp, width = cmp4.shape
    return pl.pallas_call(
        _compress_kernel,
        grid=(bsz, 4),
        in_specs=[pl.BlockSpec((1, 1, n_grp, width), lambda b, a: (b, a, 0, 0)),
                  pl.BlockSpec((1, 1, 2 * width), lambda b, a: (a // 2, 0, 0)),
                  pl.BlockSpec((1, 2 * width, CMP_HIDDEN), lambda b, a: (a // 2, 0, 0)),
                  pl.BlockSpec((1, CMP_HIDDEN, HEAD_DIM), lambda b, a: (a // 2, 0, 0))],
        out_specs=pl.BlockSpec((1, 1, n_grp, HEAD_DIM), lambda b, a: (b, a, 0, 0)),
        out_shape=jax.ShapeDtypeStruct((bsz, 4, n_grp, HEAD_DIM), BF16),
        compiler_params=_cparams(("parallel", "parallel")),
        name="compress",
    )(cmp4, pe, w1, w2)


def _reset(m_ref, acc_ref):
    m_ref[...] = jnp.full(m_ref.shape, M_INIT, F32)
    acc_ref[...] = jnp.zeros(acc_ref.shape, F32)


def _stack_heads_t(q, n_heads):
    t = q.shape[0]
    lane = lax.broadcasted_iota(I32, (t, LANES), 1)
    parts = []
    for h in range(n_heads):
        slab = q[:, (h // 2) * LANES:(h // 2 + 1) * LANES].astype(F32)
        if h % 2:
            slab = pltpu.roll(slab, HEAD_DIM, 1)
        parts.append(jnp.where(lane < HEAD_DIM, slab, 0.0).T)
    return jnp.concatenate(parts, axis=1).astype(BF16)


def _unstack_heads_t(o_t, n_heads, t):
    lane = lax.broadcasted_iota(I32, (t, LANES), 1)
    slabs = []
    for j in range(n_heads // 2):
        even = pltpu.roll(o_t[:, (2 * j) * t:(2 * j + 1) * t].T, HEAD_DIM, 1)
        slabs.append(jnp.where(lane < HEAD_DIM, even, o_t[:, (2 * j + 1) * t:(2 * j + 2) * t].T))
    return jnp.concatenate(slabs, axis=1)


def _tn_dot(a, b):
    return lax.dot_general(a, b, (((0,), (0,)), ((), ())), preferred_element_type=F32)


def _online_update_t(k_c, q_t, bias_t, v_c, n_heads, t, m_ref, acc_ref):
    s_t = jnp.dot(k_c, q_t, preferred_element_type=F32)
    ps, alphas = [], []
    for h in range(n_heads):
        cols = slice(h * t, (h + 1) * t)
        sh = s_t[:, cols]
        if bias_t is not None:
            sh = sh + bias_t
        m_old = m_ref[:, cols]
        m_new = jnp.maximum(m_old, jnp.max(sh, axis=0, keepdims=True))
        m_ref[:, cols] = m_new
        ps.append(jnp.exp2(sh - m_new).astype(BF16))
        alphas.append(jnp.exp2(m_old - m_new))
    acc_ref[...] = jnp.concatenate(alphas, axis=1) * acc_ref[...] + _tn_dot(v_c, jnp.concatenate(ps, axis=1))


INT_MIN = -2 ** 31


def _chunk_loop(n_keys, sizes, body, init):
    big, small = sizes
    n_big = n_keys // big
    carry = lax.fori_loop(0, n_big, lambda c, x: body(pl.multiple_of(c * big, big), big, x), init)
    rest = n_big * big
    n_small = (n_keys - rest + small - 1) // small
    return lax.fori_loop(0, n_small, lambda c, x: body(pl.multiple_of(rest + c * small, small), small, x), carry)


def _dsa_kernel(qa_ref, qi_ref, wq_ref, kva_ref, kiw_ref, o_ref,
                key_scr, m_scr, acc_scr, *, t, ck, k_sel):
    i = pl.program_id(1)
    t0 = i * t
    n_keys = t0 + t
    qpos = t0 + lax.broadcasted_iota(I32, (1, t), 1)

    def key_index(k0, size):
        return k0 + lax.broadcasted_iota(I32, (size, t), 0)

    qi_t = qi_ref[0].astype(F32).T
    w_t = wq_ref[0].T
    zpad = jnp.zeros((LANES - IDX_DIM, t), F32)
    qi_all = jnp.concatenate([jnp.concatenate([qi_t[h * IDX_DIM:(h + 1) * IDX_DIM], zpad], axis=0)
                              for h in range(IDX_HEADS)], axis=1).astype(BF16)

    def score_chunk(k0, size, carry):
        kic = kiw_ref[0, pl.ds(k0, size), :].astype(BF16)
        lg = jnp.dot(kic, qi_all, preferred_element_type=F32)
        acc = jnp.zeros((size, t), F32)
        for h in range(IDX_HEADS):
            acc = acc + w_t[IDX_DIM + h:IDX_DIM + h + 1] * jnp.maximum(lg[:, h * t:(h + 1) * t], 0.0)
        key_scr[pl.ds(k0, size), :] = jnp.where(key_index(k0, size) <= qpos, acc, -jnp.inf)
        return carry

    _chunk_loop(n_keys, ck, score_chunk, 0)

    n_part = 8

    def count_where(pred):
        def body(k0, size, cnt):
            hit = jnp.where(pred(key_scr[pl.ds(k0, size), :], k0, size), 1.0, 0.0)
            return cnt + jnp.sum(hit.reshape(n_part, size // (8 * n_part), 8, t), axis=1)
        cnt = _chunk_loop(n_keys, ck, body, jnp.zeros((n_part, 8, t), F32))
        return jnp.sum(jnp.sum(cnt, axis=0), axis=0, keepdims=True)

    def as_score(code):
        return pltpu.bitcast(jnp.where(code < 0, code ^ 0x7FFFFFFF, code), F32)

    def bit_step(b, lo):
        cand = lo + lax.shift_left(jnp.int32(1), 31 - b)
        v = as_score(cand)
        return jnp.where(count_where(lambda sc, k0, size: sc >= v) >= k_sel, cand, lo)

    code = lax.fori_loop(0, 32, bit_step, jnp.full((1, t), INT_MIN, I32))
    thr = jnp.where(code == INT_MIN, -jnp.inf, as_score(code))
    n_gt = count_where(lambda sc, k0, size: sc > thr)
    n_ge = count_where(lambda sc, k0, size: sc >= thr)
    need = k_sel - n_gt

    def tie_bound():
        n_bits = int(key_scr.shape[0]).bit_length()

        def step(b, j):
            cand = j + lax.shift_left(jnp.int32(1), n_bits - 1 - b)
            below = count_where(lambda sc, k0, size: (sc == thr) & (key_index(k0, size) < cand))
            return jnp.where(below <= need, cand, j)
        return lax.fori_loop(0, n_bits, step, jnp.zeros((1, t), I32))

    has_excess = jnp.max(jnp.where(n_ge > k_sel, 1.0, 0.0)) > 0.0
    jbound = lax.cond(has_excess, tie_bound, lambda: jnp.full((1, t), 2 ** 30, I32))

    q_t = _stack_heads_t(qa_ref[0], A_HEADS)
    _reset(m_scr, acc_scr)

    def attn_chunk(k0, size, carry):
        key = key_scr[pl.ds(k0, size), :]
        idx = key_index(k0, size)
        sel = ((key > thr) | ((key == thr) & (idx < jbound))) & (idx <= qpos)
        bias_t = jnp.where(sel, 0.0, MASK_NEG)
        _online_update_t(kva_ref[0, pl.ds(k0, size), :LANES], q_t, bias_t, kva_ref[0, pl.ds(k0, size), LANES:],
                         A_HEADS, t, m_scr, acc_scr)
        return carry

    _chunk_loop(n_keys, ck, attn_chunk, 0)
    o_t = acc_scr[...] / acc_scr[0:1, :]
    o_ref[0] = _unstack_heads_t(o_t, A_HEADS, t).astype(BF16)


def _dsa(qa, qi, kiw, kva, t, ck):
    bsz, seq, _ = qa.shape
    k_sel = min(DSA_TOPK_MAX, seq // 4)
    kern = functools.partial(_dsa_kernel, t=t, ck=ck, k_sel=k_sel)
    blk = lambda w: pl.BlockSpec((1, t, w), lambda b, i: (b, i, 0))
    res = lambda w: pl.BlockSpec((1, seq, w), lambda b, i: (b, 0, 0))
    cols = A_HEADS * t
    return pl.pallas_call(
        kern,
        grid=(bsz, seq // t),
        in_specs=[blk(512), blk(256), blk(LANES), res(2 * LANES), res(LANES)],
        out_specs=blk(512),
        out_shape=jax.ShapeDtypeStruct((bsz, seq, 512), BF16),
        scratch_shapes=[pltpu.VMEM((seq, t), F32), pltpu.VMEM((1, cols), F32), pltpu.VMEM((LANES, cols), F32)],
        compiler_params=_cparams(("parallel", "arbitrary")),
        name="dsa",
    )(qa, qi, kiw, kva, kiw)


def _nsa_kernel(qn_ref, qr_ref, g_ref, kcv_ref, kv_ref, c2s_ref, o_ref,
                m_scr, acc_scr, *, t, ck, n_slc, n_sel):
    kh = pl.program_id(1)
    i = pl.program_id(2)
    t0 = i * t
    qpos = t0 + lax.broadcasted_iota(I32, (1, t), 1)
    qn_t = _stack_heads_t(qn_ref[0], B_GROUP)
    qr_t = _stack_heads_t(qr_ref[0], B_GROUP)
    heads = [slice(g * t, (g + 1) * t) for g in range(B_GROUP)]

    kcv = kcv_ref[0, 0]
    n_cp = kcv.shape[0]
    s_t = jnp.dot(kcv, qn_t, preferred_element_type=F32)
    cmp_end = lax.broadcasted_iota(I32, (n_cp, t), 0) * CMP_STRIDE + (CMP_LEN - 1)
    vis = cmp_end <= qpos
    probs = []
    for cols in heads:
        sh = jnp.where(vis, s_t[:, cols], M_INIT)
        p = jnp.where(vis, jnp.exp2(sh - jnp.max(sh, axis=0, keepdims=True)), 0.0)
        den = jnp.sum(p, axis=0, keepdims=True)
        probs.append(p * jnp.where(den > 0.0, 1.0 / den, 0.0))
    o_cmp_t = _tn_dot(kcv, jnp.concatenate(probs, axis=1).astype(BF16))

    psum = probs[0]
    for g in range(1, B_GROUP):
        psum = psum + probs[g]
    p_hi = psum.astype(BF16)
    p_lo = (psum - p_hi.astype(F32)).astype(BF16)
    imp = (jnp.dot(c2s_ref[...], p_hi, preferred_element_type=F32)
           + jnp.dot(c2s_ref[...], p_lo, preferred_element_type=F32))
    jx = lax.broadcasted_iota(I32, (n_slc, t), 0)
    admiss = jx * SLC_LEN <= qpos
    cur = lax.shift_right_logical(qpos, SLC_LEN.bit_length() - 1)
    forced = admiss & ((jx == 0) | (jx == cur) | (jx == cur - 1))
    score = jnp.where(forced, jnp.inf, jnp.where(admiss, imp, -jnp.inf))
    rank = jnp.zeros((n_slc, t), F32)
    for j2 in range(n_slc):
        row = score[j2:j2 + 1, :]
        ahead = (row > score) | ((row == score) & (j2 < jx))
        rank = rank + jnp.where(ahead, 1.0, 0.0)
    selbias = jnp.where(rank < n_sel, 0.0, MASK_NEG)
    if n_slc < HEAD_DIM:
        selbias = jnp.concatenate([selbias, jnp.zeros((HEAD_DIM - n_slc, t), F32)], axis=0)
    q_aug_t = jnp.concatenate([qr_t[:HEAD_DIM], jnp.concatenate([selbias.astype(BF16)] * B_GROUP, axis=1)], axis=0)

    _reset(m_scr, acc_scr)
    def slc_chunk(k0, size, causal):
        bias_t = None
        if causal:
            bias_t = jnp.where(k0 + lax.broadcasted_iota(I32, (size, t), 0) <= qpos, 0.0, MASK_NEG)
        _online_update_t(kv_ref[0, pl.ds(k0, size), :LANES], q_aug_t, bias_t, kv_ref[0, pl.ds(k0, size), 2 * LANES:],
                         B_GROUP, t, m_scr, acc_scr)
        return 0

    diag = ck[1]
    k_diag = pl.multiple_of(t0 // diag * diag, diag)
    _chunk_loop(k_diag, ck, lambda k0, size, carry: slc_chunk(k0, size, False), 0)
    slc_chunk(k_diag, diag, True)
    o_slc_t = acc_scr[...] / acc_scr[0:1, :]

    ww = WINDOW + t
    w0 = pl.multiple_of(jnp.maximum(t0 - WINDOW, 0), t)
    kvw = kv_ref[0, pl.ds(w0, ww), LANES:2 * LANES]
    s_w = jnp.dot(kvw, qr_t, preferred_element_type=F32)
    rel = qpos - (w0 + lax.broadcasted_iota(I32, (ww, t), 0))
    bias_w = jnp.where((rel >= 0) & (rel < WINDOW), 0.0, MASK_NEG)
    pws, dens = [], []
    for cols in heads:
        sh = s_w[:, cols] + bias_w
        p = jnp.exp2(sh - jnp.max(sh, axis=0, keepdims=True))
        dens.append(jnp.sum(p, axis=0, keepdims=True))
        pws.append(p.astype(BF16))
    o_win_t = _tn_dot(kvw, jnp.concatenate(pws, axis=1)) / jnp.concatenate(dens, axis=1)

    g_t = g_ref[0].T
    grow = lax.broadcasted_iota(I32, g_t.shape, 0)
    outs = []
    for g, cols in enumerate(heads):
        hd = kh * B_GROUP + g

        def gate(branch):
            return jnp.sum(jnp.where(grow == branch * B_HEADS + hd, g_t, 0.0), axis=0, keepdims=True)

        outs.append(gate(0) * o_cmp_t[:, cols] + gate(1) * o_slc_t[:, cols] + gate(2) * o_win_t[:, cols])
    o_ref[0] = _unstack_heads_t(jnp.concatenate(outs, axis=1), B_GROUP, t).astype(BF16)


def _nsa(qbn, qbr, gsig, kcv, kvb, t, ck):
    bsz, seq, _ = qbn.shape
    n_cp = kcv.shape[2]
    n_slc = seq // SLC_LEN
    n_sel = min(SLC_TOPN, n_slc)
    assert n_slc <= HEAD_DIM and seq >= WINDOW + t and ck[1] % t == 0 and ck[0] % ck[1] == 0
    n = np.arange(n_cp)
    js = np.arange(n_slc) * SLC_LEN
    overlap = (n[None, :] * CMP_STRIDE < js[:, None] + SLC_LEN) & (n[None, :] * CMP_STRIDE + CMP_LEN > js[:, None])
    overlap &= (n[None, :] < n_cp - 1)
    c2s_t = jnp.asarray(overlap.astype(np.float32), BF16)
    kern = functools.partial(_nsa_kernel, t=t, ck=ck, n_slc=n_slc, n_sel=n_sel)
    cols = B_GROUP * t
    width = B_GROUP * HEAD_DIM
    qblk = pl.BlockSpec((1, t, width), lambda b, k, i: (b, i, k))
    return pl.pallas_call(
        kern,
        grid=(bsz, B_KV_HEADS, seq // t),
        in_specs=[qblk, qblk,
                  pl.BlockSpec((1, t, LANES), lambda b, k, i: (b, i, 0)),
                  pl.BlockSpec((1, 1, n_cp, LANES), lambda b, k, i: (b, k, 0, 0)),
                  pl.BlockSpec((1, seq, 3 * LANES), lambda b, k, i: (b, 0, k)),
                  pl.BlockSpec((n_slc, n_cp), lambda b, k, i: (0, 0))],
        out_specs=qblk,
        out_shape=jax.ShapeDtypeStruct((bsz, seq, B_HEADS * HEAD_DIM), BF16),
        scratch_shapes=[pltpu.VMEM((1, cols), F32), pltpu.VMEM((LANES, cols), F32)],
        compiler_params=_cparams(("parallel", "parallel", "arbitrary")),
        name="nsa",
    )(qbn, qbr, gsig, kcv, kvb, c2s_t)


R_EXP = 0
R_GRP = N_EXPERTS


def _merge_kernel(oa_ref, ob_ref, gate_ref, x_ref, wa_ref, wb_ref, wo_ref, g_ref, wr_ref, br_ref,
                  x1_ref, h2_ref, route_ref):
    ya = jnp.dot(oa_ref[...], wa_ref[...], preferred_element_type=F32)
    yb = jnp.dot(ob_ref[...], wb_ref[...], preferred_element_type=F32)
    merged = gate_ref[:, :D_MODEL].astype(F32) * ya + gate_ref[:, D_MODEL:].astype(F32) * yb
    x1 = x_ref[...] + jnp.dot(merged.astype(BF16), wo_ref[...], preferred_element_type=F32)
    x1_ref[...] = x1
    ms = jnp.mean(x1 * x1, axis=-1, keepdims=True)
    h2 = x1 * lax.rsqrt(ms + RMS_EPS) * g_ref[...]
    h2_ref[...] = h2

    h_hi = h2.astype(BF16)
    h_lo = (h2 - h_hi.astype(F32)).astype(BF16)
    hi_both = jnp.dot(h_hi, wr_ref[...], preferred_element_type=F32)
    logit = (hi_both[:, :LANES] + hi_both[:, LANES:]
             + jnp.dot(h_lo, wr_ref[:, :LANES], preferred_element_type=F32) + br_ref[...])
    lane = lax.broadcasted_iota(I32, logit.shape, 1)
    lane_f = lane.astype(F32)

    def first_argmax(v):
        mx = jnp.max(v, axis=-1, keepdims=True)
        return mx, jnp.min(jnp.where(v == mx, lane_f, float(LANES)), axis=-1, keepdims=True)

    is_grp = (lane >= R_GRP) & (lane < R_GRP + N_GROUPS)
    gl = jnp.where(is_grp, logit, -jnp.inf)
    gmax, garg = first_argmax(gl)
    g_sel = (garg - R_GRP).astype(I32)
    g_prob = 1.0 / jnp.sum(jnp.where(is_grp, jnp.exp(gl - gmax), 0.0), axis=-1, keepdims=True)
    grp_of_lane = lax.shift_right_logical(lane, EXPERTS_PER_GROUP.bit_length() - 1)
    in_grp = (lane < N_EXPERTS) & (grp_of_lane == g_sel)
    el = jnp.where(in_grp, logit, -jnp.inf)
    v1, i1 = first_argmax(el)
    v2, i2 = first_argmax(jnp.where(lane_f == i1, -jnp.inf, el))
    e2 = jnp.exp(v2 - v1)
    w1 = g_prob / (1.0 + e2)
    w2 = g_prob * e2 / (1.0 + e2)
    route_ref[...] = jnp.where(lane == 0, i1, jnp.where(lane == 1, i2,
                               jnp.where(lane == 2, w1, jnp.where(lane == 3, w2, 0.0))))


def _merge(o_a, o_b, gates, x2, wa, wb, wo, g, wr, br, tm):
    n = x2.shape[0]
    row = lambda w: pl.BlockSpec((tm, w), lambda i: (i, 0))
    full = lambda a: pl.BlockSpec(a.shape, lambda i: (0, 0))
    return pl.pallas_call(
        _merge_kernel,
        grid=(n // tm,),
        in_specs=[row(512), row(512), row(2048), row(D_MODEL), full(wa), full(wb), full(wo), full(g), full(wr), full(br)],
        out_specs=[row(D_MODEL), row(D_MODEL), row(LANES)],
        out_shape=[jax.ShapeDtypeStruct((n, D_MODEL), F32), jax.ShapeDtypeStruct((n, D_MODEL), F32),
                   jax.ShapeDtypeStruct((n, LANES), F32)],
        compiler_params=_cparams(("parallel",)),
        name="merge",
    )(o_a, o_b, gates, x2, wa, wb, wo, g, wr, br)


def _expert_kernel(be_ref, slot_ref, h_hbm, wgu_ref, wd_ref, y_ref, buf, sem):
    i = pl.program_id(0)

    def gather(block, slot):
        base = block * MOE_BLOCK

        def issue(r, carry):
            pltpu.make_async_copy(h_hbm.at[pl.ds(slot_ref[base + r], 1)], buf.at[slot, pl.ds(r, 1)],
                                  sem.at[slot]).start()
            return carry

        lax.fori_loop(0, MOE_BLOCK, issue, 0, unroll=8)

    @pl.when(i == 0)
    def _():
        gather(0, 0)

    @pl.when(i + 1 < pl.num_programs(0))
    def _():
        gather(i + 1, (i + 1) % 2)

    slot = i % 2
    pltpu.make_async_copy(h_hbm.at[pl.ds(0, MOE_BLOCK)], buf.at[slot], sem.at[slot]).wait()
    xb = buf[slot].astype(BF16)
    gu = jnp.dot(xb, wgu_ref[0].astype(BF16), preferred_element_type=F32)
    gate, up = gu[:, :D_EXPERT], gu[:, D_EXPERT:]
    act = gate * jax.nn.sigmoid(gate) * up
    y_ref[...] = jnp.dot(act.astype(BF16), wd_ref[0].astype(BF16), preferred_element_type=F32)


def _experts(blk_expert, slot_tok, h2, wgu, wd):
    n_blocks = blk_expert.shape[0]
    grid_spec = pltpu.PrefetchScalarGridSpec(
        num_scalar_prefetch=2,
        grid=(n_blocks,),
        in_specs=[pl.BlockSpec(memory_space=pl.ANY),
                  pl.BlockSpec((1, D_MODEL, 2 * D_EXPERT), lambda i, be, st: (be[i], 0, 0)),
                  pl.BlockSpec((1, D_EXPERT, D_MODEL), lambda i, be, st: (be[i], 0, 0))],
        out_specs=pl.BlockSpec((MOE_BLOCK, D_MODEL), lambda i, be, st: (i, 0)),
        scratch_shapes=[pltpu.VMEM((2, MOE_BLOCK, D_MODEL), F32), pltpu.SemaphoreType.DMA((2,))],
    )
    return pl.pallas_call(
        _expert_kernel,
        grid_spec=grid_spec,
        out_shape=jax.ShapeDtypeStruct((n_blocks * MOE_BLOCK, D_MODEL), F32),
        compiler_params=_cparams(("arbitrary",)),
        name="experts",
    )(blk_expert, slot_tok, h2, wgu, wd)


def _combine_kernel(dest_ref, y_hbm, x1_ref, route_ref, g_ref, o_ref, buf, sem, *, tm):
    i = pl.program_id(0)

    def gather(tile, slot):
        base = tile * tm * 2

        def issue(r, carry):
            for k in range(2):
                pltpu.make_async_copy(y_hbm.at[pl.ds(dest_ref[base + 2 * r + k], 1)],
                                      buf.at[slot, pl.ds(k * tm + r, 1)], sem.at[slot]).start()
            return carry

        lax.fori_loop(0, tm, issue, 0, unroll=4)

    @pl.when(i == 0)
    def _():
        gather(0, 0)

    @pl.when(i + 1 < pl.num_programs(0))
    def _():
        gather(i + 1, (i + 1) % 2)

    slot = i % 2
    pltpu.make_async_copy(y_hbm.at[pl.ds(0, 2 * tm)], buf.at[slot], sem.at[slot]).wait()
    route = route_ref[...]
    x = x1_ref[...] + route[:, 2:3] * buf[slot, :tm] + route[:, 3:4] * buf[slot, tm:]
    ms = jnp.mean(x * x, axis=-1, keepdims=True)
    o_ref[...] = x * lax.rsqrt(ms + RMS_EPS) * g_ref[...]


def _combine(dest, y_buf, x1, route, g, tm):
    n = x1.shape[0]
    grid_spec = pltpu.PrefetchScalarGridSpec(
        num_scalar_prefetch=1,
        grid=(n // tm,),
        in_specs=[pl.BlockSpec(memory_space=pl.ANY),
                  pl.BlockSpec((tm, D_MODEL), lambda i, d: (i, 0)),
                  pl.BlockSpec((tm, LANES), lambda i, d: (i, 0)),
                  pl.BlockSpec((1, D_MODEL), lambda i, d: (0, 0))],
        out_specs=pl.BlockSpec((tm, D_MODEL), lambda i, d: (i, 0)),
        scratch_shapes=[pltpu.VMEM((2, 2 * tm, D_MODEL), F32), pltpu.SemaphoreType.DMA((2,))],
    )
    return pl.pallas_call(
        functools.partial(_combine_kernel, tm=tm),
        grid_spec=grid_spec,
        out_shape=jax.ShapeDtypeStruct((n, D_MODEL), F32),
        compiler_params=_cparams(("arbitrary",)),
        name="combine",
    )(dest, y_buf, x1, route, g)


def _dispatch_plan(expert):
    n_tok = expert.shape[0]
    n_asg = n_tok * 2
    e_flat = expert.reshape(n_asg)
    onehot = (e_flat[:, None] == jnp.arange(N_EXPERTS, dtype=I32)[None, :]).astype(I32)
    csum = jnp.cumsum(onehot, axis=0)
    rank = jnp.take_along_axis(csum, e_flat[:, None], axis=1)[:, 0] - 1
    counts = csum[-1]
    padded = (counts + MOE_BLOCK - 1) // MOE_BLOCK * MOE_BLOCK
    pad_end = jnp.cumsum(padded)
    pad_start = pad_end - padded
    dest = (pad_start[e_flat] + rank).astype(I32)
    n_blocks = -(-n_asg // MOE_BLOCK) + N_EXPERTS
    slot_tok = jnp.zeros((n_blocks * MOE_BLOCK,), I32).at[dest].set(jnp.arange(n_asg, dtype=I32) // 2)
    blk_start = jnp.arange(n_blocks, dtype=I32) * MOE_BLOCK
    blk_expert = jnp.minimum(jnp.sum((pad_end[None, :] <= blk_start[:, None]).astype(I32), axis=1), N_EXPERTS - 1)
    return dest, slot_tok, blk_expert


def _pick(n, prefs):
    for p in prefs:
        if n % p == 0:
            return p
    return n


def _layer(x, positions, norm_mix, w_in, pe_k, w1_k, w2_k, pe_v, w1_v, w2_v, w_br_a, w_br_b, w_out,
           norm_ffn, w_group, b_group, w_expert, b_expert, w_gate_up, w_down, out_gain):
    bsz, seq, d = x.shape
    n = bsz * seq
    x2 = x.reshape(n, d)
    tables = _rope_tables(positions)

    w_p = _permute_columns(w_in.astype(BF16))
    tm = _pick(n, (512, 256, 128))
    qa, kva, qi, kiw, qbn, qbr, kvb, cmp_, gsig, gates = _inproj(x2, norm_mix[None, :], w_p, tables, tm, seq)

    n_grp = seq // CMP_STRIDE
    cmp4 = cmp_.reshape(bsz, seq, 4, HEAD_DIM).transpose(0, 2, 1, 3).reshape(bsz, 4, n_grp, CMP_STRIDE * HEAD_DIM)
    pe = jnp.stack([pe_k.reshape(1, -1), pe_v.reshape(1, -1)])
    kcvc = _compress(cmp4, pe, jnp.stack([w1_k, w1_v]).astype(BF16), jnp.stack([w2_k, w2_v]).astype(BF16))
    kcv = jnp.concatenate([kcvc[:, :B_KV_HEADS], kcvc[:, B_KV_HEADS:]], axis=-1)

    r3 = lambda a: a.reshape(bsz, seq, a.shape[-1])
    t = _pick(seq, (256, 128))
    ck = (_pick(seq, (1024, 512)), _pick(seq, (512,)))
    o_a = _dsa(r3(qa), r3(qi), r3(kiw), r3(kva), t, ck)
    o_b = _nsa(r3(qbn), r3(qbr), r3(gsig), kcv, r3(kvb), _pick(seq, (512, 256, 128)), ck)

    wr = jnp.concatenate([w_expert, w_group, jnp.zeros((d, LANES - N_EXPERTS - N_GROUPS), F32)], axis=1)
    wr_hi = wr.astype(BF16)
    wr = jnp.concatenate([wr_hi, (wr - wr_hi.astype(F32)).astype(BF16)], axis=1)
    br = jnp.zeros((1, LANES), F32).at[0, R_EXP:R_EXP + N_EXPERTS].set(b_expert).at[0, R_GRP:R_GRP + N_GROUPS].set(b_group)
    x1, h2, route = _merge(o_a.reshape(n, 512), o_b.reshape(n, 512), gates, x2,
                           w_br_a.astype(BF16), w_br_b.astype(BF16), w_out.astype(BF16),
                           norm_ffn[None, :], wr, br, _pick(n, (512, 256, 128)))

    expert = route[:, :2].astype(I32)
    dest, slot_tok, blk_expert = _dispatch_plan(expert)
    y_buf = _experts(blk_expert, slot_tok, h2, w_gate_up, w_down)
    out = _combine(dest, y_buf, x1, route, out_gain[None, :], _pick(n, (256, 128)))
    return out.reshape(bsz, seq, d)


def kernel(x, positions, norm_mix, w_in, pe_k, w1_k, w2_k, pe_v, w1_v, w2_v, w_br_a, w_br_b, w_out, norm_ffn,
           w_group, b_group, w_expert, b_expert, w_gate_up, w_down, norm_final):
    assert norm_mix.shape[0] == 1, "single-layer block"
    l = 0
    return _layer(x, positions, norm_mix[l], w_in[l], pe_k[l], w1_k[l], w2_k[l], pe_v[l], w1_v[l], w2_v[l],
                  w_br_a[l], w_br_b[l], w_out[l], norm_ffn[l], w_group[l], b_group[l], w_expert[l], b_expert[l],
                  w_gate_up[l], w_down[l], norm_final)
```

```python
import functools

import numpy as np
import jax
import jax.numpy as jnp
from jax import lax
from jax.experimental import pallas as pl
from jax.experimental.pallas import tpu as pltpu

D_MODEL = 1024
HEAD_DIM = 64
ROPE_THETA = 10000.0
RMS_EPS = 1e-6
A_HEADS = 8
IDX_HEADS = 8
IDX_DIM = 32
DSA_TOPK_MAX = 256
B_HEADS = 8
B_KV_HEADS = 2
B_GROUP = B_HEADS // B_KV_HEADS
CMP_LEN = 32
CMP_STRIDE = 16
CMP_HIDDEN = 128
SLC_LEN = 64
SLC_TOPN = 16
WINDOW = 512
N_GROUPS = 4
EXPERTS_PER_GROUP = 8
N_EXPERTS = N_GROUPS * EXPERTS_PER_GROUP
D_EXPERT = 256
MOE_BLOCK = 256

IN_SPLITS = (A_HEADS * HEAD_DIM, HEAD_DIM, HEAD_DIM, IDX_HEADS * IDX_DIM, IDX_DIM, IDX_HEADS,
             B_HEADS * HEAD_DIM, 6 * B_KV_HEADS * HEAD_DIM, 3 * B_HEADS, D_MODEL, D_MODEL)
IN_OFFS = tuple(int(v) for v in np.cumsum((0,) + IN_SPLITS)[:-1])

LANES = 128
MASK_NEG = -3e30
M_INIT = -1e30
VMEM_LIMIT = 48 * 1024 * 1024
SCALE = HEAD_DIM ** -0.5 * float(np.log2(np.e))

F32 = jnp.float32
BF16 = jnp.bfloat16
I32 = jnp.int32


def _cparams(sem):
    return pltpu.CompilerParams(dimension_semantics=sem, vmem_limit_bytes=VMEM_LIMIT)


P_QA = 0
P_KVA = 512
P_QI = 640
P_KIW = 896
P_QB = 1024
P_KVB = 1536
P_CMP = 2048
P_GB = 2304
P_GATE = 2432
P_WIDTH = 4480


def _in_perm():
    idx = -np.ones((P_WIDTH,), np.int64)
    o_qa, o_ka, o_va, o_qi, o_ki, o_wi, o_qb, o_kvb, o_gb, o_ga, o_gbb = IN_OFFS
    idx[P_QA:P_QA + 512] = o_qa + np.arange(512)
    idx[P_KVA:P_KVA + 64] = o_ka + np.arange(64)
    idx[P_KVA + 64:P_KVA + 128] = o_va + np.arange(64)
    idx[P_QI:P_QI + 256] = o_qi + np.arange(256)
    idx[P_KIW:P_KIW + 32] = o_ki + np.arange(32)
    idx[P_KIW + 32:P_KIW + 40] = o_wi + np.arange(8)
    idx[P_QB:P_QB + 512] = o_qb + np.arange(512)

    def kvb_col(s, k):
        return o_kvb + s * (B_KV_HEADS * HEAD_DIM) + k * HEAD_DIM + np.arange(HEAD_DIM)

    for k in range(B_KV_HEADS):
        for j, s in enumerate((2, 3, 4, 5)):
            lo = P_KVB + k * 256 + j * 64
            idx[lo:lo + 64] = kvb_col(s, k)
    for j, (s, k) in enumerate(((0, 0), (0, 1), (1, 0), (1, 1))):
        lo = P_CMP + j * 64
        idx[lo:lo + 64] = kvb_col(s, k)
    idx[P_GB:P_GB + 24] = o_gb + np.arange(24)
    idx[P_GATE:P_GATE + 1024] = o_ga + np.arange(1024)
    idx[P_GATE + 1024:P_GATE + 2048] = o_gbb + np.arange(1024)
    return idx


_IN_PERM = _in_perm()


def _permute_columns(w):
    pieces, start = [], 0
    for pos in range(1, P_WIDTH + 1):
        if pos < P_WIDTH:
            prev, cur = int(_IN_PERM[pos - 1]), int(_IN_PERM[pos])
            same_run = (prev < 0 and cur < 0) or (prev >= 0 and cur == prev + 1)
        if pos == P_WIDTH or not same_run:
            src = int(_IN_PERM[start])
            pieces.append(jnp.zeros((w.shape[0], pos - start), w.dtype) if src < 0 else w[:, src:src + pos - start])
            start = pos
    return jnp.concatenate(pieces, axis=1)


def _trig_kernel(pos_ref, freq_ref, cos_ref, sin_ref):
    ang = pos_ref[...] * freq_ref[0]
    cos_ref[...] = jnp.cos(ang)
    sin_ref[...] = jnp.sin(ang)


def _rope_tables(positions):
    n = positions.size
    pos = positions.reshape(n).astype(F32)
    h64, h32 = HEAD_DIM // 2, IDX_DIM // 2
    f64 = ROPE_THETA ** (-jnp.arange(h64, dtype=F32) / h64)
    f32_ = ROPE_THETA ** (-jnp.arange(h32, dtype=F32) / h32)
    rows64, rows32 = n * h64 // LANES, n * h32 // LANES
    pos_rep = jnp.concatenate([jnp.repeat(pos, h64).reshape(rows64, LANES),
                               jnp.repeat(pos, h32).reshape(rows32, LANES)], axis=0)
    freq = jnp.stack([jnp.tile(f64, LANES // h64), jnp.tile(f32_, LANES // h32)])[:, None, :]
    rows = rows64 + rows32
    tr = rows32 if rows32 <= 1024 else 1024
    assert rows64 % tr == 0 and rows32 % tr == 0
    n64_tiles = rows64 // tr
    cos, sin = pl.pallas_call(
        _trig_kernel,
        grid=(rows // tr,),
        in_specs=[pl.BlockSpec((tr, LANES), lambda i: (i, 0)),
                  pl.BlockSpec((1, 1, LANES), lambda i: (jnp.where(i >= n64_tiles, 1, 0), 0, 0))],
        out_specs=[pl.BlockSpec((tr, LANES), lambda i: (i, 0))] * 2,
        out_shape=[jax.ShapeDtypeStruct((rows, LANES), F32)] * 2,
        compiler_params=_cparams(("parallel",)),
        name="trig",
    )(pos_rep, freq)
    c64 = cos[:rows64].reshape(n, h64)
    s64 = sin[:rows64].reshape(n, h64)
    c32 = cos[rows64:].reshape(n, h32)
    s32 = sin[rows64:].reshape(n, h32)
    cos64 = jnp.tile(c64, (1, LANES // h64))
    sin64 = jnp.tile(jnp.concatenate([-s64, s64], axis=1), (1, LANES // HEAD_DIM))
    cos32 = jnp.tile(c32, (1, LANES // h32))
    sin32 = jnp.tile(jnp.concatenate([-s32, s32], axis=1), (1, LANES // IDX_DIM))
    return cos64, sin64, cos32, sin32


def _rope_slab(x, cos, sin, half):
    lane = lax.broadcasted_iota(I32, x.shape, 1)
    first = (lane % (2 * half)) < half
    rot = jnp.where(first, pltpu.roll(x, LANES - half, 1), pltpu.roll(x, half, 1))
    return x * cos + rot * sin


def _inproj_kernel(x_ref, g_ref, w_ref, c64_ref, s64_ref, c32_ref, s32_ref,
                   qa_ref, kva_ref, qi_ref, kiw_ref, qbn_ref, qbr_ref, kvb_ref, cmp_ref, gsig_ref, gate_ref, *, seq):
    x = x_ref[...]
    ms = jnp.mean(x * x, axis=-1, keepdims=True)
    h = (x * lax.rsqrt(ms + RMS_EPS) * g_ref[...]).astype(BF16)
    c64, s64, c32, s32 = c64_ref[...], s64_ref[...], c32_ref[...], s32_ref[...]
    lane = lax.broadcasted_iota(I32, c64.shape, 1)

    def proj(lo, width):
        return jnp.dot(h, w_ref[:, lo:lo + width], preferred_element_type=F32)

    def slab(p, j):
        return p[:, j * LANES:(j + 1) * LANES]

    p = proj(P_QA, 512)
    for j in range(4):
        qa_ref[:, j * LANES:(j + 1) * LANES] = (_rope_slab(slab(p, j), c64, s64, 32) * SCALE).astype(BF16)
    p = proj(P_KVA, 128)
    kva_ref[:, :LANES] = jnp.where(lane < HEAD_DIM, _rope_slab(p, c64, s64, 32), p).astype(BF16)
    kva_ref[:, LANES:] = jnp.where(lane == 0, 1.0, jnp.where(lane < HEAD_DIM, 0.0, p)).astype(BF16)
    p = proj(P_QI, 256)
    for j in range(2):
        qi_ref[:, j * LANES:(j + 1) * LANES] = _rope_slab(slab(p, j), c32, s32, 16).astype(BF16)
    p = proj(P_KIW, 128)
    kiw_ref[...] = jnp.where(lane < IDX_DIM, _rope_slab(p, c32, s32, 16), p)
    p = proj(P_QB, 512)
    qbn_ref[...] = (p * SCALE).astype(BF16)
    for j in range(4):
        qbr_ref[:, j * LANES:(j + 1) * LANES] = (_rope_slab(slab(p, j), c64, s64, 32) * SCALE).astype(BF16)
    p = proj(P_KVB, 512)
    tm = x.shape[0]
    pos = (pl.program_id(0) * tm) % seq + lax.broadcasted_iota(I32, c64.shape, 0)
    blk_hot = jnp.where(lane - HEAD_DIM == lax.shift_right_logical(pos, SLC_LEN.bit_length() - 1), 1.0, 0.0)
    for k in range(B_KV_HEADS):
        slc, win = slab(p, 2 * k), slab(p, 2 * k + 1)
        base = 3 * k * LANES
        kvb_ref[:, base:base + LANES] = jnp.where(lane < HEAD_DIM, _rope_slab(slc, c64, s64, 32), blk_hot).astype(BF16)
        kvb_ref[:, base + LANES:base + 2 * LANES] = jnp.where(lane < HEAD_DIM, _rope_slab(win, c64, s64, 32), win).astype(BF16)
        kvb_ref[:, base + 2 * LANES:base + 3 * LANES] = jnp.where(
            lane == 0, 1.0, jnp.where(lane < HEAD_DIM, 0.0, slc)).astype(BF16)
    cmp_ref[...] = proj(P_CMP, 256)
    gsig_ref[...] = jax.nn.sigmoid(proj(P_GB, 128))
    for j in range(2):
        gate_ref[:, j * 1024:(j + 1) * 1024] = jax.nn.sigmoid(proj(P_GATE + j * 1024, 1024)).astype(BF16)


def _inproj(x2, g, w_p, tables, tm, seq):
    n = x2.shape[0]
    assert seq % tm == 0
    row = lambda w: pl.BlockSpec((tm, w), lambda i: (i, 0))
    full = lambda a: pl.BlockSpec(a.shape, lambda i: (0, 0))
    outs = [(512, BF16), (256, BF16), (256, BF16), (128, F32), (512, BF16), (512, BF16), (3 * LANES * B_KV_HEADS, BF16),
            (256, F32), (128, F32), (2048, BF16)]
    return pl.pallas_call(
        functools.partial(_inproj_kernel, seq=seq),
        grid=(n // tm,),
        in_specs=[row(D_MODEL), full(g), full(w_p)] + [row(LANES)] * 4,
        out_specs=[row(w) for w, _ in outs],
        out_shape=[jax.ShapeDtypeStruct((n, w), dt) for w, dt in outs],
        compiler_params=_cparams(("parallel",)),
        name="inproj",
    )(x2, g, w_p, *tables)


def _compress_kernel(g_ref, pe_ref, w1_ref, w2_ref, o_ref):
    g = g_ref[0, 0]
    half = CMP_STRIDE * HEAD_DIM
    a = jnp.dot((g + pe_ref[0, :, :half]).astype(BF16), w1_ref[0, :half, :], preferred_element_type=F32)
    b = jnp.dot((g + pe_ref[0, :, half:]).astype(BF16), w1_ref[0, half:, :], preferred_element_type=F32)
    n_grp = g.shape[0]
    hid = a + pltpu.roll(b, n_grp - 1, 0)
    hid = hid * jax.nn.sigmoid(hid)
    o_ref[0, 0] = jnp.dot(hid.astype(BF16), w2_ref[0], preferred_element_type=F32).astype(BF16)


def _compress(cmp4, pe, w1, w2):
    bsz, _, n_grp, width = cmp4.shape
    return pl.pallas_call(
        _compress_kernel,
        grid=(bsz, 4),
        in_specs=[pl.BlockSpec((1, 1, n_grp, width), lambda b, a: (b, a, 0, 0)),
                  pl.BlockSpec((1, 1, 2 * width), lambda b, a: (a // 2, 0, 0)),
                  pl.BlockSpec((1, 2 * width, CMP_HIDDEN), lambda b, a: (a // 2, 0, 0)),
                  pl.BlockSpec((1, CMP_HIDDEN, HEAD_DIM), lambda b, a: (a // 2, 0, 0))],
        out_specs=pl.BlockSpec((1, 1, n_grp, HEAD_DIM), lambda b, a: (b, a, 0, 0)),
        out_shape=jax.ShapeDtypeStruct((bsz, 4, n_grp, HEAD_DIM), BF16),
        compiler_params=_cparams(("parallel", "parallel")),
        name="compress",
    )(cmp4, pe, w1, w2)


def _reset(m_ref, acc_ref):
    m_ref[...] = jnp.full(m_ref.shape, M_INIT, F32)
    acc_ref[...] = jnp.zeros(acc_ref.shape, F32)


def _stack_heads_t(q, n_heads):
    t = q.shape[0]
    lane = lax.broadcasted_iota(I32, (t, LANES), 1)
    parts = []
    for h in range(n_heads):
        slab = q[:, (h // 2) * LANES:(h // 2 + 1) * LANES].astype(F32)
        if h % 2:
            slab = pltpu.roll(slab, HEAD_DIM, 1)
        parts.append(jnp.where(lane < HEAD_DIM, slab, 0.0).T)
    return jnp.concatenate(parts, axis=1).astype(BF16)


def _unstack_heads_t(o_t, n_heads, t):
    lane = lax.broadcasted_iota(I32, (t, LANES), 1)
    slabs = []
    for j in range(n_heads // 2):
        even = pltpu.roll(o_t[:, (2 * j) * t:(2 * j + 1) * t].T, HEAD_DIM, 1)
        slabs.append(jnp.where(lane < HEAD_DIM, even, o_t[:, (2 * j + 1) * t:(2 * j + 2) * t].T))
    return jnp.concatenate(slabs, axis=1)


def _tn_dot(a, b):
    return lax.dot_general(a, b, (((0,), (0,)), ((), ())), preferred_element_type=F32)


def _online_update_t(k_c, q_t, bias_t, v_c, n_heads, t, m_ref, acc_ref):
    s_t = jnp.dot(k_c, q_t, preferred_element_type=F32)
    ps, alphas = [], []
    for h in range(n_heads):
        cols = slice(h * t, (h + 1) * t)
        sh = s_t[:, cols]
        if bias_t is not None:
            sh = sh + bias_t
        m_old = m_ref[:, cols]
        m_new = jnp.maximum(m_old, jnp.max(sh, axis=0, keepdims=True))
        m_ref[:, cols] = m_new
        ps.append(jnp.exp2(sh - m_new).astype(BF16))
        alphas.append(jnp.exp2(m_old - m_new))
    acc_ref[...] = jnp.concatenate(alphas, axis=1) * acc_ref[...] + _tn_dot(v_c, jnp.concatenate(ps, axis=1))


INT_MIN = -2 ** 31


def _chunk_loop(n_keys, sizes, body, init):
    big, small = sizes
    n_big = n_keys // big
    carry = lax.fori_loop(0, n_big, lambda c, x: body(pl.multiple_of(c * big, big), big, x), init)
    rest = n_big * big
    n_small = (n_keys - rest + small - 1) // small
    return lax.fori_loop(0, n_small, lambda c, x: body(pl.multiple_of(rest + c * small, small), small, x), carry)


def _dsa_kernel(qa_ref, qi_ref, wq_ref, kva_ref, kiw_ref, o_ref,
                key_scr, m_scr, acc_scr, *, t, ck, k_sel):
    i = pl.program_id(1)
    t0 = i * t
    n_keys = t0 + t
    qpos = t0 + lax.broadcasted_iota(I32, (1, t), 1)

    def key_index(k0, size):
        return k0 + lax.broadcasted_iota(I32, (size, t), 0)

    qi_t = qi_ref[0].astype(F32).T
    w_t = wq_ref[0].T
    zpad = jnp.zeros((LANES - IDX_DIM, t), F32)
    qi_all = jnp.concatenate([jnp.concatenate([qi_t[h * IDX_DIM:(h + 1) * IDX_DIM], zpad], axis=0)
                              for h in range(IDX_HEADS)], axis=1).astype(BF16)

    def score_chunk(k0, size, carry):
        kic = kiw_ref[0, pl.ds(k0, size), :].astype(BF16)
        lg = jnp.dot(kic, qi_all, preferred_element_type=F32)
        acc = jnp.zeros((size, t), F32)
        for h in range(IDX_HEADS):
            acc = acc + w_t[IDX_DIM + h:IDX_DIM + h + 1] * jnp.maximum(lg[:, h * t:(h + 1) * t], 0.0)
        key_scr[pl.ds(k0, size), :] = jnp.where(key_index(k0, size) <= qpos, acc, -jnp.inf)
        return carry

    _chunk_loop(n_keys, ck, score_chunk, 0)

    n_part = 8

    def count_where(pred):
        def body(k0, size, cnt):
            hit = jnp.where(pred(key_scr[pl.ds(k0, size), :], k0, size), 1.0, 0.0)
            return cnt + jnp.sum(hit.reshape(n_part, size // (8 * n_part), 8, t), axis=1)
        cnt = _chunk_loop(n_keys, (ck[0], t), body, jnp.zeros((n_part, 8, t), F32))
        return jnp.sum(jnp.sum(cnt, axis=0), axis=0, keepdims=True)

    def as_score(code):
        return pltpu.bitcast(jnp.where(code < 0, code ^ 0x7FFFFFFF, code), F32)

    def bit_step(b, lo):
        cand = lo + lax.shift_left(jnp.int32(1), 31 - b)
        v = as_score(cand)
        return jnp.where(count_where(lambda sc, k0, size: sc >= v) >= k_sel, cand, lo)

    code = lax.fori_loop(0, 32, bit_step, jnp.full((1, t), INT_MIN, I32))
    thr = jnp.where(code == INT_MIN, -jnp.inf, as_score(code))
    n_gt = count_where(lambda sc, k0, size: sc > thr)
    n_ge = count_where(lambda sc, k0, size: sc >= thr)
    need = k_sel - n_gt

    def tie_bound():
        n_bits = int(key_scr.shape[0]).bit_length()

        def step(b, j):
            cand = j + lax.shift_left(jnp.int32(1), n_bits - 1 - b)
            below = count_where(lambda sc, k0, size: (sc == thr) & (key_index(k0, size) < cand))
            return jnp.where(below <= need, cand, j)
        return lax.fori_loop(0, n_bits, step, jnp.zeros((1, t), I32))

    has_excess = jnp.max(jnp.where(n_ge > k_sel, 1.0, 0.0)) > 0.0
    jbound = lax.cond(has_excess, tie_bound, lambda: jnp.full((1, t), 2 ** 30, I32))

    q_t = _stack_heads_t(qa_ref[0], A_HEADS)
    _reset(m_scr, acc_scr)

    def attn_chunk(k0, size, carry):
        key = key_scr[pl.ds(k0, size), :]
        idx = key_index(k0, size)
        sel = ((key > thr) | ((key == thr) & (idx < jbound))) & (idx <= qpos)
        bias_t = jnp.where(sel, 0.0, MASK_NEG)
        _online_update_t(kva_ref[0, pl.ds(k0, size), :LANES], q_t, bias_t, kva_ref[0, pl.ds(k0, size), LANES:],
                         A_HEADS, t, m_scr, acc_scr)
        return carry

    _chunk_loop(n_keys, ck, attn_chunk, 0)
    o_t = acc_scr[...] / acc_scr[0:1, :]
    o_ref[0] = _unstack_heads_t(o_t, A_HEADS, t).astype(BF16)


def _dsa(qa, qi, kiw, kva, t, ck):
    bsz, seq, _ = qa.shape
    k_sel = min(DSA_TOPK_MAX, seq // 4)
    kern = functools.partial(_dsa_kernel, t=t, ck=ck, k_sel=k_sel)
    blk = lambda w: pl.BlockSpec((1, t, w), lambda b, i: (b, i, 0))
    res = lambda w: pl.BlockSpec((1, seq, w), lambda b, i: (b, 0, 0))
    cols = A_HEADS * t
    return pl.pallas_call(
        kern,
        grid=(bsz, seq // t),
        in_specs=[blk(512), blk(256), blk(LANES), res(2 * LANES), res(LANES)],
        out_specs=blk(512),
        out_shape=jax.ShapeDtypeStruct((bsz, seq, 512), BF16),
        scratch_shapes=[pltpu.VMEM((seq, t), F32), pltpu.VMEM((1, cols), F32), pltpu.VMEM((LANES, cols), F32)],
        compiler_params=_cparams(("parallel", "arbitrary")),
        name="dsa",
    )(qa, qi, kiw, kva, kiw)


def _nsa_kernel(qn_ref, qr_ref, g_ref, kcv_ref, kv_ref, c2s_ref, o_ref,
                m_scr, acc_scr, *, t, ck, n_slc, n_sel):
    kh = pl.program_id(1)
    i = pl.program_id(2)
    t0 = i * t
    qpos = t0 + lax.broadcasted_iota(I32, (1, t), 1)
    qn_t = _stack_heads_t(qn_ref[0], B_GROUP)
    qr_t = _stack_heads_t(qr_ref[0], B_GROUP)
    heads = [slice(g * t, (g + 1) * t) for g in range(B_GROUP)]

    kcv = kcv_ref[0, 0]
    n_cp = kcv.shape[0]
    s_t = jnp.dot(kcv, qn_t, preferred_element_type=F32)
    cmp_end = lax.broadcasted_iota(I32, (n_cp, t), 0) * CMP_STRIDE + (CMP_LEN - 1)
    vis = cmp_end <= qpos
    probs = []
    for cols in heads:
        sh = jnp.where(vis, s_t[:, cols], M_INIT)
        p = jnp.where(vis, jnp.exp2(sh - jnp.max(sh, axis=0, keepdims=True)), 0.0)
        den = jnp.sum(p, axis=0, keepdims=True)
        probs.append(p * jnp.where(den > 0.0, 1.0 / den, 0.0))
    o_cmp_t = _tn_dot(kcv, jnp.concatenate(probs, axis=1).astype(BF16))

    psum = probs[0]
    for g in range(1, B_GROUP):
        psum = psum + probs[g]
    p_hi = psum.astype(BF16)
    p_lo = (psum - p_hi.astype(F32)).astype(BF16)
    imp = (jnp.dot(c2s_ref[...], p_hi, preferred_element_type=F32)
           + jnp.dot(c2s_ref[...], p_lo, preferred_element_type=F32))
    jx = lax.broadcasted_iota(I32, (n_slc, t), 0)
    admiss = jx * SLC_LEN <= qpos
    cur = lax.shift_right_logical(qpos, SLC_LEN.bit_length() - 1)
    forced = admiss & ((jx == 0) | (jx == cur) | (jx == cur - 1))
    score = jnp.where(forced, jnp.inf, jnp.where(admiss, imp, -jnp.inf))
    rank = jnp.zeros((n_slc, t), F32)
    for j2 in range(n_slc):
        row = score[j2:j2 + 1, :]
        ahead = (row > score) | ((row == score) & (j2 < jx))
        rank = rank + jnp.where(ahead, 1.0, 0.0)
    selbias = jnp.where(rank < n_sel, 0.0, MASK_NEG)
    if n_slc < HEAD_DIM:
        selbias = jnp.concatenate([selbias, jnp.zeros((HEAD_DIM - n_slc, t), F32)], axis=0)
    q_aug_t = jnp.concatenate([qr_t[:HEAD_DIM], jnp.concatenate([selbias.astype(BF16)] * B_GROUP, axis=1)], axis=0)

    _reset(m_scr, acc_scr)
    def slc_chunk(k0, size, causal):
        bias_t = None
        if causal:
            bias_t = jnp.where(k0 + lax.broadcasted_iota(I32, (size, t), 0) <= qpos, 0.0, MASK_NEG)
        _online_update_t(kv_ref[0, pl.ds(k0, size), :LANES], q_aug_t, bias_t, kv_ref[0, pl.ds(k0, size), 2 * LANES:],
                         B_GROUP, t, m_scr, acc_scr)
        return 0

    diag = ck[1]
    k_diag = pl.multiple_of(t0 // diag * diag, diag)
    _chunk_loop(k_diag, ck, lambda k0, size, carry: slc_chunk(k0, size, False), 0)
    slc_chunk(k_diag, diag, True)
    o_slc_t = acc_scr[...] / acc_scr[0:1, :]

    ww = WINDOW + t
    w0 = pl.multiple_of(jnp.maximum(t0 - WINDOW, 0), t)
    kvw = kv_ref[0, pl.ds(w0, ww), LANES:2 * LANES]
    s_w = jnp.dot(kvw, qr_t, preferred_element_type=F32)
    rel = qpos - (w0 + lax.broadcasted_iota(I32, (ww, t), 0))
    bias_w = jnp.where((rel >= 0) & (rel < WINDOW), 0.0, MASK_NEG)
    pws, dens = [], []
    for cols in heads:
        sh = s_w[:, cols] + bias_w
        p = jnp.exp2(sh - jnp.max(sh, axis=0, keepdims=True))
        dens.append(jnp.sum(p, axis=0, keepdims=True))
        pws.append(p.astype(BF16))
    o_win_t = _tn_dot(kvw, jnp.concatenate(pws, axis=1)) / jnp.concatenate(dens, axis=1)

    g_t = g_ref[0].T
    grow = lax.broadcasted_iota(I32, g_t.shape, 0)
    outs = []
    for g, cols in enumerate(heads):
        hd = kh * B_GROUP + g

        def gate(branch):
            return jnp.sum(jnp.where(grow == branch * B_HEADS + hd, g_t, 0.0), axis=0, keepdims=True)

        outs.append(gate(0) * o_cmp_t[:, cols] + gate(1) * o_slc_t[:, cols] + gate(2) * o_win_t[:, cols])
    o_ref[0] = _unstack_heads_t(jnp.concatenate(outs, axis=1), B_GROUP, t).astype(BF16)


def _nsa(qbn, qbr, gsig, kcv, kvb, t, ck):
    bsz, seq, _ = qbn.shape
    n_cp = kcv.shape[2]
    n_slc = seq // SLC_LEN
    n_sel = min(SLC_TOPN, n_slc)
    assert n_slc <= HEAD_DIM and seq >= WINDOW + t and ck[1] % t == 0 and ck[0] % ck[1] == 0
    n = np.arange(n_cp)
    js = np.arange(n_slc) * SLC_LEN
    overlap = (n[None, :] * CMP_STRIDE < js[:, None] + SLC_LEN) & (n[None, :] * CMP_STRIDE + CMP_LEN > js[:, None])
    overlap &= (n[None, :] < n_cp - 1)
    c2s_t = jnp.asarray(overlap.astype(np.float32), BF16)
    kern = functools.partial(_nsa_kernel, t=t, ck=ck, n_slc=n_slc, n_sel=n_sel)
    cols = B_GROUP * t
    width = B_GROUP * HEAD_DIM
    qblk = pl.BlockSpec((1, t, width), lambda b, k, i: (b, i, k))
    return pl.pallas_call(
        kern,
        grid=(bsz, B_KV_HEADS, seq // t),
        in_specs=[qblk, qblk,
                  pl.BlockSpec((1, t, LANES), lambda b, k, i: (b, i, 0)),
                  pl.BlockSpec((1, 1, n_cp, LANES), lambda b, k, i: (b, k, 0, 0)),
                  pl.BlockSpec((1, seq, 3 * LANES), lambda b, k, i: (b, 0, k)),
                  pl.BlockSpec((n_slc, n_cp), lambda b, k, i: (0, 0))],
        out_specs=qblk,
        out_shape=jax.ShapeDtypeStruct((bsz, seq, B_HEADS * HEAD_DIM), BF16),
        scratch_shapes=[pltpu.VMEM((1, cols), F32), pltpu.VMEM((LANES, cols), F32)],
        compiler_params=_cparams(("parallel", "parallel", "arbitrary")),
        name="nsa",
    )(qbn, qbr, gsig, kcv, kvb, c2s_t)


R_EXP = 0
R_GRP = N_EXPERTS


def _merge_kernel(oa_ref, ob_ref, gate_ref, x_ref, wa_ref, wb_ref, wo_ref, g_ref, wr_ref, br_ref,
                  x1_ref, h2_ref, route_ref):
    ya = jnp.dot(oa_ref[...], wa_ref[...], preferred_element_type=F32)
    yb = jnp.dot(ob_ref[...], wb_ref[...], preferred_element_type=F32)
    merged = gate_ref[:, :D_MODEL].astype(F32) * ya + gate_ref[:, D_MODEL:].astype(F32) * yb
    x1 = x_ref[...] + jnp.dot(merged.astype(BF16), wo_ref[...], preferred_element_type=F32)
    x1_ref[...] = x1
    ms = jnp.mean(x1 * x1, axis=-1, keepdims=True)
    h2 = x1 * lax.rsqrt(ms + RMS_EPS) * g_ref[...]
    h2_ref[...] = h2

    h_hi = h2.astype(BF16)
    h_lo = (h2 - h_hi.astype(F32)).astype(BF16)
    hi_both = jnp.dot(h_hi, wr_ref[...], preferred_element_type=F32)
    logit = (hi_both[:, :LANES] + hi_both[:, LANES:]
             + jnp.dot(h_lo, wr_ref[:, :LANES], preferred_element_type=F32) + br_ref[...])
    lane = lax.broadcasted_iota(I32, logit.shape, 1)
    lane_f = lane.astype(F32)

    def first_argmax(v):
        mx = jnp.max(v, axis=-1, keepdims=True)
        return mx, jnp.min(jnp.where(v == mx, lane_f, float(LANES)), axis=-1, keepdims=True)

    is_grp = (lane >= R_GRP) & (lane < R_GRP + N_GROUPS)
    gl = jnp.where(is_grp, logit, -jnp.inf)
    gmax, garg = first_argmax(gl)
    g_sel = (garg - R_GRP).astype(I32)
    g_prob = 1.0 / jnp.sum(jnp.where(is_grp, jnp.exp(gl - gmax), 0.0), axis=-1, keepdims=True)
    grp_of_lane = lax.shift_right_logical(lane, EXPERTS_PER_GROUP.bit_length() - 1)
    in_grp = (lane < N_EXPERTS) & (grp_of_lane == g_sel)
    el = jnp.where(in_grp, logit, -jnp.inf)
    v1, i1 = first_argmax(el)
    v2, i2 = first_argmax(jnp.where(lane_f == i1, -jnp.inf, el))
    e2 = jnp.exp(v2 - v1)
    w1 = g_prob / (1.0 + e2)
    w2 = g_prob * e2 / (1.0 + e2)
    route_ref[...] = jnp.where(lane == 0, i1, jnp.where(lane == 1, i2,
                               jnp.where(lane == 2, w1, jnp.where(lane == 3, w2, 0.0))))


def _merge(o_a, o_b, gates, x2, wa, wb, wo, g, wr, br, tm):
    n = x2.shape[0]
    row = lambda w: pl.BlockSpec((tm, w), lambda i: (i, 0))
    full = lambda a: pl.BlockSpec(a.shape, lambda i: (0, 0))
    return pl.pallas_call(
        _merge_kernel,
        grid=(n // tm,),
        in_specs=[row(512), row(512), row(2048), row(D_MODEL), full(wa), full(wb), full(wo), full(g), full(wr), full(br)],
        out_specs=[row(D_MODEL), row(D_MODEL), row(LANES)],
        out_shape=[jax.ShapeDtypeStruct((n, D_MODEL), F32), jax.ShapeDtypeStruct((n, D_MODEL), F32),
                   jax.ShapeDtypeStruct((n, LANES), F32)],
        compiler_params=_cparams(("parallel",)),
        name="merge",
    )(o_a, o_b, gates, x2, wa, wb, wo, g, wr, br)


def _expert_kernel(be_ref, slot_ref, h_hbm, wgu_ref, wd_ref, y_ref, buf, sem, wgu_bf, wd_bf):
    i = pl.program_id(0)
    n_batch = 4
    rows = MOE_BLOCK // n_batch

    def gather(block, slot, j):
        base = block * MOE_BLOCK + j * rows

        def issue(r, carry):
            pltpu.make_async_copy(h_hbm.at[pl.ds(slot_ref[base + r], 1)], buf.at[slot, pl.ds(j * rows + r, 1)],
                                  sem.at[slot]).start()
            return carry

        lax.fori_loop(0, rows, issue, 0, unroll=8)

    def prefetch(j):
        @pl.when(i + 1 < pl.num_programs(0))
        def _():
            gather(i + 1, (i + 1) % 2, j)

    @pl.when(i == 0)
    def _():
        for j in range(n_batch):
            gather(0, 0, j)

    @pl.when((i == 0) | (be_ref[i] != be_ref[jnp.maximum(i - 1, 0)]))
    def _():
        wgu_bf[...] = wgu_ref[0].astype(BF16)
        wd_bf[...] = wd_ref[0].astype(BF16)

    slot = i % 2
    pltpu.make_async_copy(h_hbm.at[pl.ds(0, MOE_BLOCK)], buf.at[slot], sem.at[slot]).wait()
    xb = buf[slot].astype(BF16)
    prefetch(0)
    gate = jnp.dot(xb, wgu_bf[:, :D_EXPERT], preferred_element_type=F32)
    prefetch(1)
    up = jnp.dot(xb, wgu_bf[:, D_EXPERT:], preferred_element_type=F32)
    act = (gate * jax.nn.sigmoid(gate) * up).astype(BF16)
    half = D_MODEL // 2
    prefetch(2)
    y_ref[:, :half] = jnp.dot(act, wd_bf[:, :half], preferred_element_type=F32)
    prefetch(3)
    y_ref[:, half:] = jnp.dot(act, wd_bf[:, half:], preferred_element_type=F32)


def _experts(blk_expert, slot_tok, h2, wgu, wd):
    n_blocks = blk_expert.shape[0]
    grid_spec = pltpu.PrefetchScalarGridSpec(
        num_scalar_prefetch=2,
        grid=(n_blocks,),
        in_specs=[pl.BlockSpec(memory_space=pl.ANY),
                  pl.BlockSpec((1, D_MODEL, 2 * D_EXPERT), lambda i, be, st: (be[i], 0, 0)),
                  pl.BlockSpec((1, D_EXPERT, D_MODEL), lambda i, be, st: (be[i], 0, 0))],
        out_specs=pl.BlockSpec((MOE_BLOCK, D_MODEL), lambda i, be, st: (i, 0)),
        scratch_shapes=[pltpu.VMEM((2, MOE_BLOCK, D_MODEL), F32), pltpu.SemaphoreType.DMA((2,)),
                        pltpu.VMEM((D_MODEL, 2 * D_EXPERT), BF16), pltpu.VMEM((D_EXPERT, D_MODEL), BF16)],
    )
    return pl.pallas_call(
        _expert_kernel,
        grid_spec=grid_spec,
        out_shape=jax.ShapeDtypeStruct((n_blocks * MOE_BLOCK, D_MODEL), F32),
        compiler_params=_cparams(("arbitrary",)),
        name="experts",
    )(blk_expert, slot_tok, h2, wgu, wd)


def _combine_kernel(dest_ref, y_hbm, x1_ref, route_ref, g_ref, o_ref, buf, sem, *, tm):
    i = pl.program_id(0)

    def gather(tile, slot):
        base = tile * tm * 2

        def issue(r, carry):
            for k in range(2):
                pltpu.make_async_copy(y_hbm.at[pl.ds(dest_ref[base + 2 * r + k], 1)],
                                      buf.at[slot, pl.ds(k * tm + r, 1)], sem.at[slot]).start()
            return carry

        lax.fori_loop(0, tm, issue, 0, unroll=4)

    @pl.when(i == 0)
    def _():
        gather(0, 0)

    @pl.when(i + 1 < pl.num_programs(0))
    def _():
        gather(i + 1, (i + 1) % 2)

    slot = i % 2
    pltpu.make_async_copy(y_hbm.at[pl.ds(0, 2 * tm)], buf.at[slot], sem.at[slot]).wait()
    route = route_ref[...]
    x = x1_ref[...] + route[:, 2:3] * buf[slot, :tm] + route[:, 3:4] * buf[slot, tm:]
    ms = jnp.mean(x * x, axis=-1, keepdims=True)
    o_ref[...] = x * lax.rsqrt(ms + RMS_EPS) * g_ref[...]


def _combine(dest, y_buf, x1, route, g, tm):
    n = x1.shape[0]
    grid_spec = pltpu.PrefetchScalarGridSpec(
        num_scalar_prefetch=1,
        grid=(n // tm,),
        in_specs=[pl.BlockSpec(memory_space=pl.ANY),
                  pl.BlockSpec((tm, D_MODEL), lambda i, d: (i, 0)),
                  pl.BlockSpec((tm, LANES), lambda i, d: (i, 0)),
                  pl.BlockSpec((1, D_MODEL), lambda i, d: (0, 0))],
        out_specs=pl.BlockSpec((tm, D_MODEL), lambda i, d: (i, 0)),
        scratch_shapes=[pltpu.VMEM((2, 2 * tm, D_MODEL), F32), pltpu.SemaphoreType.DMA((2,))],
    )
    return pl.pallas_call(
        functools.partial(_combine_kernel, tm=tm),
        grid_spec=grid_spec,
        out_shape=jax.ShapeDtypeStruct((n, D_MODEL), F32),
        compiler_params=_cparams(("arbitrary",)),
        name="combine",
    )(dest, y_buf, x1, route, g)


def _dispatch_plan(expert):
    n_tok = expert.shape[0]
    n_asg = n_tok * 2
    e_flat = expert.reshape(n_asg)
    onehot = (e_flat[:, None] == jnp.arange(N_EXPERTS, dtype=I32)[None, :]).astype(I32)
    csum = jnp.cumsum(onehot, axis=0)
    rank = jnp.take_along_axis(csum, e_flat[:, None], axis=1)[:, 0] - 1
    counts = csum[-1]
    padded = (counts + MOE_BLOCK - 1) // MOE_BLOCK * MOE_BLOCK
    pad_end = jnp.cumsum(padded)
    pad_start = pad_end - padded
    dest = (pad_start[e_flat] + rank).astype(I32)
    n_blocks = -(-n_asg // MOE_BLOCK) + N_EXPERTS
    slot_tok = jnp.zeros((n_blocks * MOE_BLOCK,), I32).at[dest].set(jnp.arange(n_asg, dtype=I32) // 2)
    blk_start = jnp.arange(n_blocks, dtype=I32) * MOE_BLOCK
    blk_expert = jnp.minimum(jnp.sum((pad_end[None, :] <= blk_start[:, None]).astype(I32), axis=1), N_EXPERTS - 1)
    return dest, slot_tok, blk_expert


def _pick(n, prefs):
    for p in prefs:
        if n % p == 0:
            return p
    return n


def _layer(x, positions, norm_mix, w_in, pe_k, w1_k, w2_k, pe_v, w1_v, w2_v, w_br_a, w_br_b, w_out,
           norm_ffn, w_group, b_group, w_expert, b_expert, w_gate_up, w_down, out_gain):
    bsz, seq, d = x.shape
    n = bsz * seq
    x2 = x.reshape(n, d)
    tables = _rope_tables(positions)

    w_p = _permute_columns(w_in.astype(BF16))
    tm = _pick(n, (512, 256, 128))
    qa, kva, qi, kiw, qbn, qbr, kvb, cmp_, gsig, gates = _inproj(x2, norm_mix[None, :], w_p, tables, tm, seq)

    n_grp = seq // CMP_STRIDE
    cmp4 = cmp_.reshape(bsz, seq, 4, HEAD_DIM).transpose(0, 2, 1, 3).reshape(bsz, 4, n_grp, CMP_STRIDE * HEAD_DIM)
    pe = jnp.stack([pe_k.reshape(1, -1), pe_v.reshape(1, -1)])
    kcvc = _compress(cmp4, pe, jnp.stack([w1_k, w1_v]).astype(BF16), jnp.stack([w2_k, w2_v]).astype(BF16))
    kcv = jnp.concatenate([kcvc[:, :B_KV_HEADS], kcvc[:, B_KV_HEADS:]], axis=-1)

    r3 = lambda a: a.reshape(bsz, seq, a.shape[-1])
    t = _pick(seq, (256, 128))
    ck = (_pick(seq, (1024, 512)), _pick(seq, (512,)))
    o_a = _dsa(r3(qa), r3(qi), r3(kiw), r3(kva), t, ck)
    o_b = _nsa(r3(qbn), r3(qbr), r3(gsig), kcv, r3(kvb), _pick(seq, (512, 256, 128)), ck)

    wr = jnp.concatenate([w_expert, w_group, jnp.zeros((d, LANES - N_EXPERTS - N_GROUPS), F32)], axis=1)
    wr_hi = wr.astype(BF16)
    wr = jnp.concatenate([wr_hi, (wr - wr_hi.astype(F32)).astype(BF16)], axis=1)
    br = jnp.zeros((1, LANES), F32).at[0, R_EXP:R_EXP + N_EXPERTS].set(b_expert).at[0, R_GRP:R_GRP + N_GROUPS].set(b_group)
    x1, h2, route = _merge(o_a.reshape(n, 512), o_b.reshape(n, 512), gates, x2,
                           w_br_a.astype(BF16), w_br_b.astype(BF16), w_out.astype(BF16),
                           norm_ffn[None, :], wr, br, _pick(n, (512, 256, 128)))

    expert = route[:, :2].astype(I32)
    dest, slot_tok, blk_expert = _dispatch_plan(expert)
    y_buf = _experts(blk_expert, slot_tok, h2, w_gate_up, w_down)
    out = _combine(dest, y_buf, x1, route, out_gain[None, :], _pick(n, (256, 128)))
    return out.reshape(bsz, seq, d)


def kernel(x, positions, norm_mix, w_in, pe_k, w1_k, w2_k, pe_v, w1_v, w2_v, w_br_a, w_br_b, w_out, norm_ffn,
           w_group, b_group, w_expert, b_expert, w_gate_up, w_down, norm_final):
    assert norm_mix.shape[0] == 1, "single-layer block"
    l = 0
    return _layer(x, positions, norm_mix[l], w_in[l], pe_k[l], w1_k[l], w2_k[l], pe_v[l], w1_v[l], w2_v[l],
                  w_br_a[l], w_br_b[l], w_out[l], norm_ffn[l], w_group[l], b_group[l], w_expert[l], b_expert[l],
                  w_gate_up[l], w_down[l], norm_final)
```

```python
import functools

import numpy as np
import jax
import jax.numpy as jnp
from jax import lax
from jax.experimental import pallas as pl
from jax.experimental.pallas import tpu as pltpu

D_MODEL = 1024
HEAD_DIM = 64
ROPE_THETA = 10000.0
RMS_EPS = 1e-6
A_HEADS = 8
IDX_HEADS = 8
IDX_DIM = 32
DSA_TOPK_MAX = 256
B_HEADS = 8
B_KV_HEADS = 2
B_GROUP = B_HEADS // B_KV_HEADS
CMP_LEN = 32
CMP_STRIDE = 16
CMP_HIDDEN = 128
SLC_LEN = 64
SLC_TOPN = 16
WINDOW = 512
N_GROUPS = 4
EXPERTS_PER_GROUP = 8
N_EXPERTS = N_GROUPS * EXPERTS_PER_GROUP
D_EXPERT = 256
MOE_BLOCK = 256

IN_SPLITS = (A_HEADS * HEAD_DIM, HEAD_DIM, HEAD_DIM, IDX_HEADS * IDX_DIM, IDX_DIM, IDX_HEADS,
             B_HEADS * HEAD_DIM, 6 * B_KV_HEADS * HEAD_DIM, 3 * B_HEADS, D_MODEL, D_MODEL)
IN_OFFS = tuple(int(v) for v in np.cumsum((0,) + IN_SPLITS)[:-1])

LANES = 128
MASK_NEG = -3e30
M_INIT = -1e30
VMEM_LIMIT = 48 * 1024 * 1024
SCALE = HEAD_DIM ** -0.5 * float(np.log2(np.e))

F32 = jnp.float32
BF16 = jnp.bfloat16
I32 = jnp.int32


def _cparams(sem):
    return pltpu.CompilerParams(dimension_semantics=sem, vmem_limit_bytes=VMEM_LIMIT)


P_QA = 0
P_KVA = 512
P_QI = 640
P_KIW = 896
P_QB = 1024
P_KVB = 1536
P_CMP = 2048
P_GB = 2304
P_GATE = 2432
P_WIDTH = 4480


def _in_perm():
    idx = -np.ones((P_WIDTH,), np.int64)
    o_qa, o_ka, o_va, o_qi, o_ki, o_wi, o_qb, o_kvb, o_gb, o_ga, o_gbb = IN_OFFS
    idx[P_QA:P_QA + 512] = o_qa + np.arange(512)
    idx[P_KVA:P_KVA + 64] = o_ka + np.arange(64)
    idx[P_KVA + 64:P_KVA + 128] = o_va + np.arange(64)
    idx[P_QI:P_QI + 256] = o_qi + np.arange(256)
    idx[P_KIW:P_KIW + 32] = o_ki + np.arange(32)
    idx[P_KIW + 32:P_KIW + 40] = o_wi + np.arange(8)
    idx[P_QB:P_QB + 512] = o_qb + np.arange(512)

    def kvb_col(s, k):
        return o_kvb + s * (B_KV_HEADS * HEAD_DIM) + k * HEAD_DIM + np.arange(HEAD_DIM)

    for k in range(B_KV_HEADS):
        for j, s in enumerate((2, 3, 4, 5)):
            lo = P_KVB + k * 256 + j * 64
            idx[lo:lo + 64] = kvb_col(s, k)
    for j, (s, k) in enumerate(((0, 0), (0, 1), (1, 0), (1, 1))):
        lo = P_CMP + j * 64
        idx[lo:lo + 64] = kvb_col(s, k)
    idx[P_GB:P_GB + 24] = o_gb + np.arange(24)
    idx[P_GATE:P_GATE + 1024] = o_ga + np.arange(1024)
    idx[P_GATE + 1024:P_GATE + 2048] = o_gbb + np.arange(1024)
    return idx


_IN_PERM = _in_perm()


def _permute_columns(w):
    pieces, start = [], 0
    for pos in range(1, P_WIDTH + 1):
        if pos < P_WIDTH:
            prev, cur = int(_IN_PERM[pos - 1]), int(_IN_PERM[pos])
            same_run = (prev < 0 and cur < 0) or (prev >= 0 and cur == prev + 1)
        if pos == P_WIDTH or not same_run:
            src = int(_IN_PERM[start])
            pieces.append(jnp.zeros((w.shape[0], pos - start), w.dtype) if src < 0 else w[:, src:src + pos - start])
            start = pos
    return jnp.concatenate(pieces, axis=1)


def _trig_kernel(pos_ref, freq_ref, cos_ref, sin_ref):
    ang = pos_ref[...] * freq_ref[0]
    cos_ref[...] = jnp.cos(ang)
    sin_ref[...] = jnp.sin(ang)


def _rope_tables(positions):
    n = positions.size
    pos = positions.reshape(n).astype(F32)
    h64, h32 = HEAD_DIM // 2, IDX_DIM // 2
    f64 = ROPE_THETA ** (-jnp.arange(h64, dtype=F32) / h64)
    f32_ = ROPE_THETA ** (-jnp.arange(h32, dtype=F32) / h32)
    rows64, rows32 = n * h64 // LANES, n * h32 // LANES
    pos_rep = jnp.concatenate([jnp.repeat(pos, h64).reshape(rows64, LANES),
                               jnp.repeat(pos, h32).reshape(rows32, LANES)], axis=0)
    freq = jnp.stack([jnp.tile(f64, LANES // h64), jnp.tile(f32_, LANES // h32)])[:, None, :]
    rows = rows64 + rows32
    tr = rows32 if rows32 <= 1024 else 1024
    assert rows64 % tr == 0 and rows32 % tr == 0
    n64_tiles = rows64 // tr
    cos, sin = pl.pallas_call(
        _trig_kernel,
        grid=(rows // tr,),
        in_specs=[pl.BlockSpec((tr, LANES), lambda i: (i, 0)),
                  pl.BlockSpec((1, 1, LANES), lambda i: (jnp.where(i >= n64_tiles, 1, 0), 0, 0))],
        out_specs=[pl.BlockSpec((tr, LANES), lambda i: (i, 0))] * 2,
        out_shape=[jax.ShapeDtypeStruct((rows, LANES), F32)] * 2,
        compiler_params=_cparams(("parallel",)),
        name="trig",
    )(pos_rep, freq)
    c64 = cos[:rows64].reshape(n, h64)
    s64 = sin[:rows64].reshape(n, h64)
    c32 = cos[rows64:].reshape(n, h32)
    s32 = sin[rows64:].reshape(n, h32)
    cos64 = jnp.tile(c64, (1, LANES // h64))
    sin64 = jnp.tile(jnp.concatenate([-s64, s64], axis=1), (1, LANES // HEAD_DIM))
    cos32 = jnp.tile(c32, (1, LANES // h32))
    sin32 = jnp.tile(jnp.concatenate([-s32, s32], axis=1), (1, LANES // IDX_DIM))
    return cos64, sin64, cos32, sin32


def _rope_slab(x, cos, sin, half):
    lane = lax.broadcasted_iota(I32, x.shape, 1)
    first = (lane % (2 * half)) < half
    rot = jnp.where(first, pltpu.roll(x, LANES - half, 1), pltpu.roll(x, half, 1))
    return x * cos + rot * sin


def _inproj_kernel(x_ref, g_ref, w_ref, c64_ref, s64_ref, c32_ref, s32_ref,
                   qa_ref, kva_ref, qi_ref, kiw_ref, qbn_ref, qbr_ref, kvb_ref, cmp_ref, gsig_ref, gate_ref, *, seq):
    x = x_ref[...]
    ms = jnp.mean(x * x, axis=-1, keepdims=True)
    h = (x * lax.rsqrt(ms + RMS_EPS) * g_ref[...]).astype(BF16)
    c64, s64, c32, s32 = c64_ref[...], s64_ref[...], c32_ref[...], s32_ref[...]
    lane = lax.broadcasted_iota(I32, c64.shape, 1)

    def proj(lo, width):
        return jnp.dot(h, w_ref[:, lo:lo + width], preferred_element_type=F32)

    def slab(p, j):
        return p[:, j * LANES:(j + 1) * LANES]

    p = proj(P_QA, 512)
    for j in range(4):
        qa_ref[:, j * LANES:(j + 1) * LANES] = (_rope_slab(slab(p, j), c64, s64, 32) * SCALE).astype(BF16)
    p = proj(P_KVA, 128)
    kva_ref[:, :LANES] = jnp.where(lane < HEAD_DIM, _rope_slab(p, c64, s64, 32), p).astype(BF16)
    kva_ref[:, LANES:] = jnp.where(lane == 0, 1.0, jnp.where(lane < HEAD_DIM, 0.0, p)).astype(BF16)
    p = proj(P_QI, 256)
    for j in range(2):
        qi_ref[:, j * LANES:(j + 1) * LANES] = _rope_slab(slab(p, j), c32, s32, 16).astype(BF16)
    p = proj(P_KIW, 128)
    kiw_ref[...] = jnp.where(lane < IDX_DIM, _rope_slab(p, c32, s32, 16), p)
    p = proj(P_QB, 512)
    qbn_ref[...] = (p * SCALE).astype(BF16)
    for j in range(4):
        qbr_ref[:, j * LANES:(j + 1) * LANES] = (_rope_slab(slab(p, j), c64, s64, 32) * SCALE).astype(BF16)
    p = proj(P_KVB, 512)
    tm = x.shape[0]
    pos = (pl.program_id(0) * tm) % seq + lax.broadcasted_iota(I32, c64.shape, 0)
    blk_hot = jnp.where(lane - HEAD_DIM == lax.shift_right_logical(pos, SLC_LEN.bit_length() - 1), 1.0, 0.0)
    for k in range(B_KV_HEADS):
        slc, win = slab(p, 2 * k), slab(p, 2 * k + 1)
        base = 3 * k * LANES
        kvb_ref[:, base:base + LANES] = jnp.where(lane < HEAD_DIM, _rope_slab(slc, c64, s64, 32), blk_hot).astype(BF16)
        kvb_ref[:, base + LANES:base + 2 * LANES] = jnp.where(lane < HEAD_DIM, _rope_slab(win, c64, s64, 32), win).astype(BF16)
        kvb_ref[:, base + 2 * LANES:base + 3 * LANES] = jnp.where(
            lane == 0, 1.0, jnp.where(lane < HEAD_DIM, 0.0, slc)).astype(BF16)
    cmp_ref[...] = proj(P_CMP, 256)
    gsig_ref[...] = jax.nn.sigmoid(proj(P_GB, 128))
    for j in range(2):
        gate_ref[:, j * 1024:(j + 1) * 1024] = jax.nn.sigmoid(proj(P_GATE + j * 1024, 1024)).astype(BF16)


def _inproj(x2, g, w_p, tables, tm, seq):
    n = x2.shape[0]
    assert seq % tm == 0
    row = lambda w: pl.BlockSpec((tm, w), lambda i: (i, 0))
    full = lambda a: pl.BlockSpec(a.shape, lambda i: (0, 0))
    outs = [(512, BF16), (256, BF16), (256, BF16), (128, F32), (512, BF16), (512, BF16), (3 * LANES * B_KV_HEADS, BF16),
            (256, F32), (128, F32), (2048, BF16)]
    return pl.pallas_call(
        functools.partial(_inproj_kernel, seq=seq),
        grid=(n // tm,),
        in_specs=[row(D_MODEL), full(g), full(w_p)] + [row(LANES)] * 4,
        out_specs=[row(w) for w, _ in outs],
        out_shape=[jax.ShapeDtypeStruct((n, w), dt) for w, dt in outs],
        compiler_params=_cparams(("parallel",)),
        name="inproj",
    )(x2, g, w_p, *tables)


def _compress_kernel(g_ref, pe_ref, w1_ref, w2_ref, o_ref):
    g = g_ref[0, 0]
    half = CMP_STRIDE * HEAD_DIM
    a = jnp.dot((g + pe_ref[0, :, :half]).astype(BF16), w1_ref[0, :half, :], preferred_element_type=F32)
    b = jnp.dot((g + pe_ref[0, :, half:]).astype(BF16), w1_ref[0, half:, :], preferred_element_type=F32)
    n_grp = g.shape[0]
    hid = a + pltpu.roll(b, n_grp - 1, 0)
    hid = hid * jax.nn.sigmoid(hid)
    o_ref[0, 0] = jnp.dot(hid.astype(BF16), w2_ref[0], preferred_element_type=F32).astype(BF16)


def _compress(cmp4, pe, w1, w2):
    bsz, _, n_grp, width = cmp4.shape
    return pl.pallas_call(
        _compress_kernel,
        grid=(bsz, 4),
        in_specs=[pl.BlockSpec((1, 1, n_grp, width), lambda b, a: (b, a, 0, 0)),
                  pl.BlockSpec((1, 1, 2 * width), lambda b, a: (a // 2, 0, 0)),
                  pl.BlockSpec((1, 2 * width, CMP_HIDDEN), lambda b, a: (a // 2, 0, 0)),
                  pl.BlockSpec((1, CMP_HIDDEN, HEAD_DIM), lambda b, a: (a // 2, 0, 0))],
        out_specs=pl.BlockSpec((1, 1, n_grp, HEAD_DIM), lambda b, a: (b, a, 0, 0)),
        out_shape=jax.ShapeDtypeStruct((bsz, 4, n_grp, HEAD_DIM), BF16),
        compiler_params=_cparams(("parallel", "parallel")),
        name="compress",
    )(cmp4, pe, w1, w2)


def _reset(m_ref, acc_ref):
    m_ref[...] = jnp.full(m_ref.shape, M_INIT, F32)
    acc_ref[...] = jnp.zeros(acc_ref.shape, F32)


def _stack_heads_t(q, n_heads):
    t = q.shape[0]
    lane = lax.broadcasted_iota(I32, (t, LANES), 1)
    parts = []
    for h in range(n_heads):
        slab = q[:, (h // 2) * LANES:(h // 2 + 1) * LANES].astype(F32)
        if h % 2:
            slab = pltpu.roll(slab, HEAD_DIM, 1)
        parts.append(jnp.where(lane < HEAD_DIM, slab, 0.0).T)
    return jnp.concatenate(parts, axis=1).astype(BF16)


def _unstack_heads_t(o_t, n_heads, t):
    lane = lax.broadcasted_iota(I32, (t, LANES), 1)
    slabs = []
    for j in range(n_heads // 2):
        even = pltpu.roll(o_t[:, (2 * j) * t:(2 * j + 1) * t].T, HEAD_DIM, 1)
        slabs.append(jnp.where(lane < HEAD_DIM, even, o_t[:, (2 * j + 1) * t:(2 * j + 2) * t].T))
    return jnp.concatenate(slabs, axis=1)


def _tn_dot(a, b):
    return lax.dot_general(a, b, (((0,), (0,)), ((), ())), preferred_element_type=F32)


def _online_update_t(k_c, q_t, bias_t, v_c, n_heads, t, m_ref, acc_ref):
    s_t = jnp.dot(k_c, q_t, preferred_element_type=F32)
    ps, alphas = [], []
    for h in range(n_heads):
        cols = slice(h * t, (h + 1) * t)
        sh = s_t[:, cols]
        if bias_t is not None:
            sh = sh + bias_t
        m_old = m_ref[:, cols]
        m_new = jnp.maximum(m_old, jnp.max(sh, axis=0, keepdims=True))
        m_ref[:, cols] = m_new
        ps.append(jnp.exp2(sh - m_new).astype(BF16))
        alphas.append(jnp.exp2(m_old - m_new))
    acc_ref[...] = jnp.concatenate(alphas, axis=1) * acc_ref[...] + _tn_dot(v_c, jnp.concatenate(ps, axis=1))


INT_MIN = -2 ** 31


def _chunk_loop(n_keys, sizes, body, init):
    big, small = sizes
    n_big = n_keys // big
    carry = lax.fori_loop(0, n_big, lambda c, x: body(pl.multiple_of(c * big, big), big, x), init)
    rest = n_big * big
    n_small = (n_keys - rest + small - 1) // small
    return lax.fori_loop(0, n_small, lambda c, x: body(pl.multiple_of(rest + c * small, small), small, x), carry)


def _dsa_kernel(qa_ref, qi_ref, wq_ref, kva_ref, kiw_ref, o_ref,
                key_scr, m_scr, acc_scr, *, t, ck, k_sel):
    i = pl.program_id(1)
    t0 = i * t
    n_keys = t0 + t
    qpos = t0 + lax.broadcasted_iota(I32, (1, t), 1)

    def key_index(k0, size):
        return k0 + lax.broadcasted_iota(I32, (size, t), 0)

    qi_t = qi_ref[0].astype(F32).T
    w_t = wq_ref[0].T
    zpad = jnp.zeros((LANES - IDX_DIM, t), F32)
    qi_all = jnp.concatenate([jnp.concatenate([qi_t[h * IDX_DIM:(h + 1) * IDX_DIM], zpad], axis=0)
                              for h in range(IDX_HEADS)], axis=1).astype(BF16)

    def score_chunk(k0, size, carry):
        kic = kiw_ref[0, pl.ds(k0, size), :].astype(BF16)
        lg = jnp.dot(kic, qi_all, preferred_element_type=F32)
        acc = jnp.zeros((size, t), F32)
        for h in range(IDX_HEADS):
            acc = acc + w_t[IDX_DIM + h:IDX_DIM + h + 1] * jnp.maximum(lg[:, h * t:(h + 1) * t], 0.0)
        key_scr[pl.ds(k0, size), :] = jnp.where(key_index(k0, size) <= qpos, acc, -jnp.inf)
        return carry

    _chunk_loop(n_keys, ck, score_chunk, 0)

    n_part = 8

    def count_where(pred):
        def body(k0, size, cnt):
            hit = jnp.where(pred(key_scr[pl.ds(k0, size), :], k0, size), 1.0, 0.0)
            return cnt + jnp.sum(hit.reshape(n_part, size // (8 * n_part), 8, t), axis=1)
        cnt = _chunk_loop(n_keys, (ck[0], t), body, jnp.zeros((n_part, 8, t), F32))
        return jnp.sum(jnp.sum(cnt, axis=0), axis=0, keepdims=True)

    def as_score(code):
        return pltpu.bitcast(jnp.where(code < 0, code ^ 0x7FFFFFFF, code), F32)

    def bit_step(b, lo):
        cand = lo + lax.shift_left(jnp.int32(1), 31 - b)
        v = as_score(cand)
        return jnp.where(count_where(lambda sc, k0, size: sc >= v) >= k_sel, cand, lo)

    code = lax.fori_loop(0, 32, bit_step, jnp.full((1, t), INT_MIN, I32))
    thr = jnp.where(code == INT_MIN, -jnp.inf, as_score(code))
    n_gt = count_where(lambda sc, k0, size: sc > thr)
    n_ge = count_where(lambda sc, k0, size: sc >= thr)
    need = k_sel - n_gt

    def tie_bound():
        n_bits = int(key_scr.shape[0]).bit_length()

        def step(b, j):
            cand = j + lax.shift_left(jnp.int32(1), n_bits - 1 - b)
            below = count_where(lambda sc, k0, size: (sc == thr) & (key_index(k0, size) < cand))
            return jnp.where(below <= need, cand, j)
        return lax.fori_loop(0, n_bits, step, jnp.zeros((1, t), I32))

    has_excess = jnp.max(jnp.where(n_ge > k_sel, 1.0, 0.0)) > 0.0
    jbound = lax.cond(has_excess, tie_bound, lambda: jnp.full((1, t), 2 ** 30, I32))

    q_t = _stack_heads_t(qa_ref[0], A_HEADS)
    _reset(m_scr, acc_scr)

    def attn_chunk(k0, size, carry):
        key = key_scr[pl.ds(k0, size), :]
        idx = key_index(k0, size)
        sel = ((key > thr) | ((key == thr) & (idx < jbound))) & (idx <= qpos)
        bias_t = jnp.where(sel, 0.0, MASK_NEG)
        _online_update_t(kva_ref[0, pl.ds(k0, size), :LANES], q_t, bias_t, kva_ref[0, pl.ds(k0, size), LANES:],
                         A_HEADS, t, m_scr, acc_scr)
        return carry

    _chunk_loop(n_keys, ck, attn_chunk, 0)
    o_t = acc_scr[...] / acc_scr[0:1, :]
    o_ref[0] = _unstack_heads_t(o_t, A_HEADS, t).astype(BF16)


def _dsa(qa, qi, kiw, kva, t, ck):
    bsz, seq, _ = qa.shape
    k_sel = min(DSA_TOPK_MAX, seq // 4)
    kern = functools.partial(_dsa_kernel, t=t, ck=ck, k_sel=k_sel)
    blk = lambda w: pl.BlockSpec((1, t, w), lambda b, i: (b, i, 0))
    res = lambda w: pl.BlockSpec((1, seq, w), lambda b, i: (b, 0, 0))
    cols = A_HEADS * t
    return pl.pallas_call(
        kern,
        grid=(bsz, seq // t),
        in_specs=[blk(512), blk(256), blk(LANES), res(2 * LANES), res(LANES)],
        out_specs=blk(512),
        out_shape=jax.ShapeDtypeStruct((bsz, seq, 512), BF16),
        scratch_shapes=[pltpu.VMEM((seq, t), F32), pltpu.VMEM((1, cols), F32), pltpu.VMEM((LANES, cols), F32)],
        compiler_params=_cparams(("parallel", "arbitrary")),
        name="dsa",
    )(qa, qi, kiw, kva, kiw)


def _nsa_kernel(qn_ref, qr_ref, g_ref, kcv_ref, kv_ref, c2s_ref, o_ref,
                m_scr, acc_scr, *, t, ck, n_slc, n_sel):
    kh = pl.program_id(1)
    i = pl.program_id(2)
    t0 = i * t
    qpos = t0 + lax.broadcasted_iota(I32, (1, t), 1)
    qn_t = _stack_heads_t(qn_ref[0], B_GROUP)
    qr_t = _stack_heads_t(qr_ref[0], B_GROUP)
    heads = [slice(g * t, (g + 1) * t) for g in range(B_GROUP)]

    kcv = kcv_ref[0, 0]
    n_cp = kcv.shape[0]
    s_t = jnp.dot(kcv, qn_t, preferred_element_type=F32)
    cmp_end = lax.broadcasted_iota(I32, (n_cp, t), 0) * CMP_STRIDE + (CMP_LEN - 1)
    vis = cmp_end <= qpos
    probs = []
    for cols in heads:
        sh = jnp.where(vis, s_t[:, cols], M_INIT)
        p = jnp.where(vis, jnp.exp2(sh - jnp.max(sh, axis=0, keepdims=True)), 0.0)
        den = jnp.sum(p, axis=0, keepdims=True)
        probs.append(p * jnp.where(den > 0.0, 1.0 / den, 0.0))
    o_cmp_t = _tn_dot(kcv, jnp.concatenate(probs, axis=1).astype(BF16))

    psum = probs[0]
    for g in range(1, B_GROUP):
        psum = psum + probs[g]
    p_hi = psum.astype(BF16)
    p_lo = (psum - p_hi.astype(F32)).astype(BF16)
    imp = (jnp.dot(c2s_ref[...], p_hi, preferred_element_type=F32)
           + jnp.dot(c2s_ref[...], p_lo, preferred_element_type=F32))
    jx = lax.broadcasted_iota(I32, (n_slc, t), 0)
    admiss = jx * SLC_LEN <= qpos
    cur = lax.shift_right_logical(qpos, SLC_LEN.bit_length() - 1)
    forced = admiss & ((jx == 0) | (jx == cur) | (jx == cur - 1))
    score = jnp.where(forced, jnp.inf, jnp.where(admiss, imp, -jnp.inf))
    rank = jnp.zeros((n_slc, t), F32)
    for j2 in range(n_slc):
        row = score[j2:j2 + 1, :]
        ahead = (row > score) | ((row == score) & (j2 < jx))
        rank = rank + jnp.where(ahead, 1.0, 0.0)
    selbias = jnp.where(rank < n_sel, 0.0, MASK_NEG)
    if n_slc < HEAD_DIM:
        selbias = jnp.concatenate([selbias, jnp.zeros((HEAD_DIM - n_slc, t), F32)], axis=0)
    q_aug_t = jnp.concatenate([qr_t[:HEAD_DIM], jnp.concatenate([selbias.astype(BF16)] * B_GROUP, axis=1)], axis=0)

    _reset(m_scr, acc_scr)
    def slc_chunk(k0, size, causal):
        bias_t = None
        if causal:
            bias_t = jnp.where(k0 + lax.broadcasted_iota(I32, (size, t), 0) <= qpos, 0.0, MASK_NEG)
        _online_update_t(kv_ref[0, pl.ds(k0, size), :LANES], q_aug_t, bias_t, kv_ref[0, pl.ds(k0, size), 2 * LANES:],
                         B_GROUP, t, m_scr, acc_scr)
        return 0

    diag = ck[1]
    k_diag = pl.multiple_of(t0 // diag * diag, diag)
    _chunk_loop(k_diag, ck, lambda k0, size, carry: slc_chunk(k0, size, False), 0)
    slc_chunk(k_diag, diag, True)
    o_slc_t = acc_scr[...] / acc_scr[0:1, :]

    ww = WINDOW + t
    w0 = pl.multiple_of(jnp.maximum(t0 - WINDOW, 0), t)
    kvw = kv_ref[0, pl.ds(w0, ww), LANES:2 * LANES]
    s_w = jnp.dot(kvw, qr_t, preferred_element_type=F32)
    rel = qpos - (w0 + lax.broadcasted_iota(I32, (ww, t), 0))
    bias_w = jnp.where((rel >= 0) & (rel < WINDOW), 0.0, MASK_NEG)
    pws, dens = [], []
    for cols in heads:
        sh = s_w[:, cols] + bias_w
        p = jnp.exp2(sh - jnp.max(sh, axis=0, keepdims=True))
        dens.append(jnp.sum(p, axis=0, keepdims=True))
        pws.append(p.astype(BF16))
    o_win_t = _tn_dot(kvw, jnp.concatenate(pws, axis=1)) / jnp.concatenate(dens, axis=1)

    g_t = g_ref[0].T
    grow = lax.broadcasted_iota(I32, g_t.shape, 0)
    outs = []
    for g, cols in enumerate(heads):
        hd = kh * B_GROUP + g

        def gate(branch):
            return jnp.sum(jnp.where(grow == branch * B_HEADS + hd, g_t, 0.0), axis=0, keepdims=True)

        outs.append(gate(0) * o_cmp_t[:, cols] + gate(1) * o_slc_t[:, cols] + gate(2) * o_win_t[:, cols])
    o_ref[0] = _unstack_heads_t(jnp.concatenate(outs, axis=1), B_GROUP, t).astype(BF16)


def _nsa(qbn, qbr, gsig, kcv, kvb, t, ck):
    bsz, seq, _ = qbn.shape
    n_cp = kcv.shape[2]
    n_slc = seq // SLC_LEN
    n_sel = min(SLC_TOPN, n_slc)
    assert n_slc <= HEAD_DIM and seq >= WINDOW + t and ck[1] % t == 0 and ck[0] % ck[1] == 0
    n = np.arange(n_cp)
    js = np.arange(n_slc) * SLC_LEN
    overlap = (n[None, :] * CMP_STRIDE < js[:, None] + SLC_LEN) & (n[None, :] * CMP_STRIDE + CMP_LEN > js[:, None])
    overlap &= (n[None, :] < n_cp - 1)
    c2s_t = jnp.asarray(overlap.astype(np.float32), BF16)
    kern = functools.partial(_nsa_kernel, t=t, ck=ck, n_slc=n_slc, n_sel=n_sel)
    cols = B_GROUP * t
    width = B_GROUP * HEAD_DIM
    qblk = pl.BlockSpec((1, t, width), lambda b, k, i: (b, i, k))
    return pl.pallas_call(
        kern,
        grid=(bsz, B_KV_HEADS, seq // t),
        in_specs=[qblk, qblk,
                  pl.BlockSpec((1, t, LANES), lambda b, k, i: (b, i, 0)),
                  pl.BlockSpec((1, 1, n_cp, LANES), lambda b, k, i: (b, k, 0, 0)),
                  pl.BlockSpec((1, seq, 3 * LANES), lambda b, k, i: (b, 0, k)),
                  pl.BlockSpec((n_slc, n_cp), lambda b, k, i: (0, 0))],
        out_specs=qblk,
        out_shape=jax.ShapeDtypeStruct((bsz, seq, B_HEADS * HEAD_DIM), BF16),
        scratch_shapes=[pltpu.VMEM((1, cols), F32), pltpu.VMEM((LANES, cols), F32)],
        compiler_params=_cparams(("parallel", "parallel", "arbitrary")),
        name="nsa",
    )(qbn, qbr, gsig, kcv, kvb, c2s_t)


R_EXP = 0
R_GRP = N_EXPERTS


def _merge_kernel(oa_ref, ob_ref, gate_ref, x_ref, wa_ref, wb_ref, wo_ref, g_ref, wr_ref, br_ref,
                  x1_ref, h2_ref, route_ref):
    ya = jnp.dot(oa_ref[...], wa_ref[...], preferred_element_type=F32)
    yb = jnp.dot(ob_ref[...], wb_ref[...], preferred_element_type=F32)
    merged = gate_ref[:, :D_MODEL].astype(F32) * ya + gate_ref[:, D_MODEL:].astype(F32) * yb
    x1 = x_ref[...] + jnp.dot(merged.astype(BF16), wo_ref[...], preferred_element_type=F32)
    x1_ref[...] = x1
    ms = jnp.mean(x1 * x1, axis=-1, keepdims=True)
    h2 = x1 * lax.rsqrt(ms + RMS_EPS) * g_ref[...]
    _to_tiles(h2_ref, h2)

    h_hi = h2.astype(BF16)
    h_lo = (h2 - h_hi.astype(F32)).astype(BF16)
    hi_both = jnp.dot(h_hi, wr_ref[...], preferred_element_type=F32)
    logit = (hi_both[:, :LANES] + hi_both[:, LANES:]
             + jnp.dot(h_lo, wr_ref[:, :LANES], preferred_element_type=F32) + br_ref[...])
    lane = lax.broadcasted_iota(I32, logit.shape, 1)
    lane_f = lane.astype(F32)

    def first_argmax(v):
        mx = jnp.max(v, axis=-1, keepdims=True)
        return mx, jnp.min(jnp.where(v == mx, lane_f, float(LANES)), axis=-1, keepdims=True)

    is_grp = (lane >= R_GRP) & (lane < R_GRP + N_GROUPS)
    gl = jnp.where(is_grp, logit, -jnp.inf)
    gmax, garg = first_argmax(gl)
    g_sel = (garg - R_GRP).astype(I32)
    g_prob = 1.0 / jnp.sum(jnp.where(is_grp, jnp.exp(gl - gmax), 0.0), axis=-1, keepdims=True)
    grp_of_lane = lax.shift_right_logical(lane, EXPERTS_PER_GROUP.bit_length() - 1)
    in_grp = (lane < N_EXPERTS) & (grp_of_lane == g_sel)
    el = jnp.where(in_grp, logit, -jnp.inf)
    v1, i1 = first_argmax(el)
    v2, i2 = first_argmax(jnp.where(lane_f == i1, -jnp.inf, el))
    e2 = jnp.exp(v2 - v1)
    w1 = g_prob / (1.0 + e2)
    w2 = g_prob * e2 / (1.0 + e2)
    route_ref[...] = jnp.where(lane == 0, i1, jnp.where(lane == 1, i2,
                               jnp.where(lane == 2, w1, jnp.where(lane == 3, w2, 0.0))))


def _merge(o_a, o_b, gates, x2, wa, wb, wo, g, wr, br, tm):
    n = x2.shape[0]
    row = lambda w: pl.BlockSpec((tm, w), lambda i: (i, 0))
    full = lambda a: pl.BlockSpec(a.shape, lambda i: (0, 0))
    return pl.pallas_call(
        _merge_kernel,
        grid=(n // tm,),
        in_specs=[row(512), row(512), row(2048), row(D_MODEL), full(wa), full(wb), full(wo), full(g), full(wr), full(br)],
        out_specs=[row(D_MODEL), pl.BlockSpec((tm,) + TILE_ROW, lambda i: (i, 0, 0)), row(LANES)],
        out_shape=[jax.ShapeDtypeStruct((n, D_MODEL), F32), jax.ShapeDtypeStruct((n,) + TILE_ROW, F32),
                   jax.ShapeDtypeStruct((n, LANES), F32)],
        compiler_params=_cparams(("parallel",)),
        name="merge",
    )(o_a, o_b, gates, x2, wa, wb, wo, g, wr, br)


SUBLANES = 8
TILE_ROW = (SUBLANES, LANES)
assert SUBLANES * LANES == D_MODEL


def _to_tiles(ref, x):
    for s in range(SUBLANES):
        ref[:, s, :] = x[:, s * LANES:(s + 1) * LANES]


def _from_tiles(ref):
    return jnp.concatenate([ref[:, s, :] for s in range(SUBLANES)], axis=1)


def _expert_kernel(be_ref, slot_ref, h_hbm, wgu_ref, wd_ref, y_ref, buf, sem, wgu_bf, wd_bf):
    i = pl.program_id(0)

    def gather(block, slot):
        base = block * MOE_BLOCK

        def issue(r, carry):
            pltpu.make_async_copy(h_hbm.at[slot_ref[base + r]], buf.at[slot, r], sem.at[slot]).start()
            return carry

        lax.fori_loop(0, MOE_BLOCK, issue, 0, unroll=8)

    @pl.when(i == 0)
    def _():
        gather(0, 0)

    @pl.when(i + 1 < pl.num_programs(0))
    def _():
        gather(i + 1, (i + 1) % 2)

    @pl.when((i == 0) | (be_ref[i] != be_ref[jnp.maximum(i - 1, 0)]))
    def _():
        wgu_bf[...] = wgu_ref[0].astype(BF16)
        wd_bf[...] = wd_ref[0].astype(BF16)

    slot = i % 2
    pltpu.make_async_copy(h_hbm.at[pl.ds(0, MOE_BLOCK)], buf.at[slot], sem.at[slot]).wait()
    xb = _from_tiles(buf.at[slot]).astype(BF16)
    gu = jnp.dot(xb, wgu_bf[...], preferred_element_type=F32)
    gate, up = gu[:, :D_EXPERT], gu[:, D_EXPERT:]
    act = (gate * jax.nn.sigmoid(gate) * up).astype(BF16)
    _to_tiles(y_ref, jnp.dot(act, wd_bf[...], preferred_element_type=F32))


def _experts(blk_expert, slot_tok, h2, wgu, wd):
    n_blocks = blk_expert.shape[0]
    grid_spec = pltpu.PrefetchScalarGridSpec(
        num_scalar_prefetch=2,
        grid=(n_blocks,),
        in_specs=[pl.BlockSpec(memory_space=pl.ANY),
                  pl.BlockSpec((1, D_MODEL, 2 * D_EXPERT), lambda i, be, st: (be[i], 0, 0)),
                  pl.BlockSpec((1, D_EXPERT, D_MODEL), lambda i, be, st: (be[i], 0, 0))],
        out_specs=pl.BlockSpec((MOE_BLOCK,) + TILE_ROW, lambda i, be, st: (i, 0, 0)),
        scratch_shapes=[pltpu.VMEM((2, MOE_BLOCK) + TILE_ROW, F32), pltpu.SemaphoreType.DMA((2,)),
                        pltpu.VMEM((D_MODEL, 2 * D_EXPERT), BF16), pltpu.VMEM((D_EXPERT, D_MODEL), BF16)],
    )
    return pl.pallas_call(
        _expert_kernel,
        grid_spec=grid_spec,
        out_shape=jax.ShapeDtypeStruct((n_blocks * MOE_BLOCK,) + TILE_ROW, F32),
        compiler_params=_cparams(("arbitrary",)),
        name="experts",
    )(blk_expert, slot_tok, h2, wgu, wd)


def _combine_kernel(dest_ref, y_hbm, x1_ref, route_ref, g_ref, o_ref, buf, sem, *, tm):
    i = pl.program_id(0)

    def gather(tile, slot):
        base = tile * tm * 2

        def issue(r, carry):
            for k in range(2):
                pltpu.make_async_copy(y_hbm.at[dest_ref[base + 2 * r + k]],
                                      buf.at[slot, k * tm + r], sem.at[slot]).start()
            return carry

        lax.fori_loop(0, tm, issue, 0, unroll=4)

    @pl.when(i == 0)
    def _():
        gather(0, 0)

    @pl.when(i + 1 < pl.num_programs(0))
    def _():
        gather(i + 1, (i + 1) % 2)

    slot = i % 2
    pltpu.make_async_copy(y_hbm.at[pl.ds(0, 2 * tm)], buf.at[slot], sem.at[slot]).wait()
    route = route_ref[...]
    x = (x1_ref[...] + route[:, 2:3] * _from_tiles(buf.at[slot, pl.ds(0, tm)])
         + route[:, 3:4] * _from_tiles(buf.at[slot, pl.ds(tm, tm)]))
    ms = jnp.mean(x * x, axis=-1, keepdims=True)
    o_ref[...] = x * lax.rsqrt(ms + RMS_EPS) * g_ref[...]


def _combine(dest, y_buf, x1, route, g, tm):
    n = x1.shape[0]
    grid_spec = pltpu.PrefetchScalarGridSpec(
        num_scalar_prefetch=1,
        grid=(n // tm,),
        in_specs=[pl.BlockSpec(memory_space=pl.ANY),
                  pl.BlockSpec((tm, D_MODEL), lambda i, d: (i, 0)),
                  pl.BlockSpec((tm, LANES), lambda i, d: (i, 0)),
                  pl.BlockSpec((1, D_MODEL), lambda i, d: (0, 0))],
        out_specs=pl.BlockSpec((tm, D_MODEL), lambda i, d: (i, 0)),
        scratch_shapes=[pltpu.VMEM((2, 2 * tm) + TILE_ROW, F32), pltpu.SemaphoreType.DMA((2,))],
    )
    return pl.pallas_call(
        functools.partial(_combine_kernel, tm=tm),
        grid_spec=grid_spec,
        out_shape=jax.ShapeDtypeStruct((n, D_MODEL), F32),
        compiler_params=_cparams(("arbitrary",)),
        name="combine",
    )(dest, y_buf, x1, route, g)


def _dispatch_plan(expert):
    n_tok = expert.shape[0]
    n_asg = n_tok * 2
    e_flat = expert.reshape(n_asg)
    onehot = (e_flat[:, None] == jnp.arange(N_EXPERTS, dtype=I32)[None, :]).astype(I32)
    csum = jnp.cumsum(onehot, axis=0)
    rank = jnp.take_along_axis(csum, e_flat[:, None], axis=1)[:, 0] - 1
    counts = csum[-1]
    padded = (counts + MOE_BLOCK - 1) // MOE_BLOCK * MOE_BLOCK
    pad_end = jnp.cumsum(padded)
    pad_start = pad_end - padded
    dest = (pad_start[e_flat] + rank).astype(I32)
    n_blocks = -(-n_asg // MOE_BLOCK) + N_EXPERTS
    slot_tok = jnp.zeros((n_blocks * MOE_BLOCK,), I32).at[dest].set(jnp.arange(n_asg, dtype=I32) // 2)
    blk_start = jnp.arange(n_blocks, dtype=I32) * MOE_BLOCK
    blk_expert = jnp.minimum(jnp.sum((pad_end[None, :] <= blk_start[:, None]).astype(I32), axis=1), N_EXPERTS - 1)
    return dest, slot_tok, blk_expert


def _pick(n, prefs):
    for p in prefs:
        if n % p == 0:
            return p
    return n


def _layer(x, positions, norm_mix, w_in, pe_k, w1_k, w2_k, pe_v, w1_v, w2_v, w_br_a, w_br_b, w_out,
           norm_ffn, w_group, b_group, w_expert, b_expert, w_gate_up, w_down, out_gain):
    bsz, seq, d = x.shape
    n = bsz * seq
    x2 = x.reshape(n, d)
    tables = _rope_tables(positions)

    w_p = _permute_columns(w_in.astype(BF16))
    tm = _pick(n, (512, 256, 128))
    qa, kva, qi, kiw, qbn, qbr, kvb, cmp_, gsig, gates = _inproj(x2, norm_mix[None, :], w_p, tables, tm, seq)

    n_grp = seq // CMP_STRIDE
    cmp4 = cmp_.reshape(bsz, seq, 4, HEAD_DIM).transpose(0, 2, 1, 3).reshape(bsz, 4, n_grp, CMP_STRIDE * HEAD_DIM)
    pe = jnp.stack([pe_k.reshape(1, -1), pe_v.reshape(1, -1)])
    kcvc = _compress(cmp4, pe, jnp.stack([w1_k, w1_v]).astype(BF16), jnp.stack([w2_k, w2_v]).astype(BF16))
    kcv = jnp.concatenate([kcvc[:, :B_KV_HEADS], kcvc[:, B_KV_HEADS:]], axis=-1)

    r3 = lambda a: a.reshape(bsz, seq, a.shape[-1])
    t = _pick(seq, (256, 128))
    ck = (_pick(seq, (1024, 512)), _pick(seq, (512,)))
    o_a = _dsa(r3(qa), r3(qi), r3(kiw), r3(kva), t, ck)
    o_b = _nsa(r3(qbn), r3(qbr), r3(gsig), kcv, r3(kvb), _pick(seq, (512, 256, 128)), ck)

    wr = jnp.concatenate([w_expert, w_group, jnp.zeros((d, LANES - N_EXPERTS - N_GROUPS), F32)], axis=1)
    wr_hi = wr.astype(BF16)
    wr = jnp.concatenate([wr_hi, (wr - wr_hi.astype(F32)).astype(BF16)], axis=1)
    br = jnp.zeros((1, LANES), F32).at[0, R_EXP:R_EXP + N_EXPERTS].set(b_expert).at[0, R_GRP:R_GRP + N_GROUPS].set(b_group)
    x1, h2, route = _merge(o_a.reshape(n, 512), o_b.reshape(n, 512), gates, x2,
                           w_br_a.astype(BF16), w_br_b.astype(BF16), w_out.astype(BF16),
                           norm_ffn[None, :], wr, br, _pick(n, (512, 256, 128)))

    expert = route[:, :2].astype(I32)
    dest, slot_tok, blk_expert = _dispatch_plan(expert)
    y_buf = _experts(blk_expert, slot_tok, h2, w_gate_up, w_down)
    out = _combine(dest, y_buf, x1, route, out_gain[None, :], _pick(n, (256, 128)))
    return out.reshape(bsz, seq, d)


def kernel(x, positions, norm_mix, w_in, pe_k, w1_k, w2_k, pe_v, w1_v, w2_v, w_br_a, w_br_b, w_out, norm_ffn,
           w_group, b_group, w_expert, b_expert, w_gate_up, w_down, norm_final):
    assert norm_mix.shape[0] == 1, "single-layer block"
    l = 0
    return _layer(x, positions, norm_mix[l], w_in[l], pe_k[l], w1_k[l], w2_k[l], pe_v[l], w1_v[l], w2_v[l],
                  w_br_a[l], w_br_b[l], w_out[l], norm_ffn[l], w_group[l], b_group[l], w_expert[l], b_expert[l],
                  w_gate_up[l], w_down[l], norm_final)
```

```python
import functools

import numpy as np
import jax
import jax.numpy as jnp
from jax import lax
from jax.experimental import pallas as pl
from jax.experimental.pallas import tpu as pltpu

D_MODEL = 1024
HEAD_DIM = 64
ROPE_THETA = 10000.0
RMS_EPS = 1e-6
A_HEADS = 8
IDX_HEADS = 8
IDX_DIM = 32
DSA_TOPK_MAX = 256
B_HEADS = 8
B_KV_HEADS = 2
B_GROUP = B_HEADS // B_KV_HEADS
CMP_LEN = 32
CMP_STRIDE = 16
CMP_HIDDEN = 128
SLC_LEN = 64
SLC_TOPN = 16
WINDOW = 512
N_GROUPS = 4
EXPERTS_PER_GROUP = 8
N_EXPERTS = N_GROUPS * EXPERTS_PER_GROUP
D_EXPERT = 256
MOE_BLOCK = 256

IN_SPLITS = (A_HEADS * HEAD_DIM, HEAD_DIM, HEAD_DIM, IDX_HEADS * IDX_DIM, IDX_DIM, IDX_HEADS,
             B_HEADS * HEAD_DIM, 6 * B_KV_HEADS * HEAD_DIM, 3 * B_HEADS, D_MODEL, D_MODEL)
IN_OFFS = tuple(int(v) for v in np.cumsum((0,) + IN_SPLITS)[:-1])

LANES = 128
MASK_NEG = -3e30
M_INIT = -1e30
VMEM_LIMIT = 48 * 1024 * 1024
SCALE = HEAD_DIM ** -0.5 * float(np.log2(np.e))

F32 = jnp.float32
BF16 = jnp.bfloat16
I32 = jnp.int32


def _cparams(sem):
    return pltpu.CompilerParams(dimension_semantics=sem, vmem_limit_bytes=VMEM_LIMIT)


P_QA = 0
P_KVA = 512
P_QI = 640
P_KIW = 896
P_QB = 1024
P_KVB = 1536
P_CMP = 2048
P_GB = 2304
P_GATE = 2432
P_WIDTH = 4480


def _in_perm():
    idx = -np.ones((P_WIDTH,), np.int64)
    o_qa, o_ka, o_va, o_qi, o_ki, o_wi, o_qb, o_kvb, o_gb, o_ga, o_gbb = IN_OFFS
    idx[P_QA:P_QA + 512] = o_qa + np.arange(512)
    idx[P_KVA:P_KVA + 64] = o_ka + np.arange(64)
    idx[P_KVA + 64:P_KVA + 128] = o_va + np.arange(64)
    idx[P_QI:P_QI + 256] = o_qi + np.arange(256)
    idx[P_KIW:P_KIW + 32] = o_ki + np.arange(32)
    idx[P_KIW + 32:P_KIW + 40] = o_wi + np.arange(8)
    idx[P_QB:P_QB + 512] = o_qb + np.arange(512)

    def kvb_col(s, k):
        return o_kvb + s * (B_KV_HEADS * HEAD_DIM) + k * HEAD_DIM + np.arange(HEAD_DIM)

    for k in range(B_KV_HEADS):
        for j, s in enumerate((2, 3, 4, 5)):
            lo = P_KVB + k * 256 + j * 64
            idx[lo:lo + 64] = kvb_col(s, k)
    for j, (s, k) in enumerate(((0, 0), (0, 1), (1, 0), (1, 1))):
        lo = P_CMP + j * 64
        idx[lo:lo + 64] = kvb_col(s, k)
    idx[P_GB:P_GB + 24] = o_gb + np.arange(24)
    idx[P_GATE:P_GATE + 1024] = o_ga + np.arange(1024)
    idx[P_GATE + 1024:P_GATE + 2048] = o_gbb + np.arange(1024)
    return idx


_IN_PERM = _in_perm()


def _permute_columns(w):
    pieces, start = [], 0
    for pos in range(1, P_WIDTH + 1):
        if pos < P_WIDTH:
            prev, cur = int(_IN_PERM[pos - 1]), int(_IN_PERM[pos])
            same_run = (prev < 0 and cur < 0) or (prev >= 0 and cur == prev + 1)
        if pos == P_WIDTH or not same_run:
            src = int(_IN_PERM[start])
            pieces.append(jnp.zeros((w.shape[0], pos - start), w.dtype) if src < 0 else w[:, src:src + pos - start])
            start = pos
    return jnp.concatenate(pieces, axis=1)


def _trig_kernel(pos_ref, freq_ref, cos_ref, sin_ref):
    ang = pos_ref[...] * freq_ref[0]
    cos_ref[...] = jnp.cos(ang)
    sin_ref[...] = jnp.sin(ang)


def _rope_tables(positions):
    n = positions.size
    pos = positions.reshape(n).astype(F32)
    h64, h32 = HEAD_DIM // 2, IDX_DIM // 2
    f64 = ROPE_THETA ** (-jnp.arange(h64, dtype=F32) / h64)
    f32_ = ROPE_THETA ** (-jnp.arange(h32, dtype=F32) / h32)
    rows64, rows32 = n * h64 // LANES, n * h32 // LANES
    pos_rep = jnp.concatenate([jnp.repeat(pos, h64).reshape(rows64, LANES),
                               jnp.repeat(pos, h32).reshape(rows32, LANES)], axis=0)
    freq = jnp.stack([jnp.tile(f64, LANES // h64), jnp.tile(f32_, LANES // h32)])[:, None, :]
    rows = rows64 + rows32
    tr = rows32 if rows32 <= 1024 else 1024
    assert rows64 % tr == 0 and rows32 % tr == 0
    n64_tiles = rows64 // tr
    cos, sin = pl.pallas_call(
        _trig_kernel,
        grid=(rows // tr,),
        in_specs=[pl.BlockSpec((tr, LANES), lambda i: (i, 0)),
                  pl.BlockSpec((1, 1, LANES), lambda i: (jnp.where(i >= n64_tiles, 1, 0), 0, 0))],
        out_specs=[pl.BlockSpec((tr, LANES), lambda i: (i, 0))] * 2,
        out_shape=[jax.ShapeDtypeStruct((rows, LANES), F32)] * 2,
        compiler_params=_cparams(("parallel",)),
        name="trig",
    )(pos_rep, freq)
    c64 = cos[:rows64].reshape(n, h64)
    s64 = sin[:rows64].reshape(n, h64)
    c32 = cos[rows64:].reshape(n, h32)
    s32 = sin[rows64:].reshape(n, h32)
    cos64 = jnp.tile(c64, (1, LANES // h64))
    sin64 = jnp.tile(jnp.concatenate([-s64, s64], axis=1), (1, LANES // HEAD_DIM))
    cos32 = jnp.tile(c32, (1, LANES // h32))
    sin32 = jnp.tile(jnp.concatenate([-s32, s32], axis=1), (1, LANES // IDX_DIM))
    return cos64, sin64, cos32, sin32


def _rope_slab(x, cos, sin, half):
    lane = lax.broadcasted_iota(I32, x.shape, 1)
    first = (lane % (2 * half)) < half
    rot = jnp.where(first, pltpu.roll(x, LANES - half, 1), pltpu.roll(x, half, 1))
    return x * cos + rot * sin


def _inproj_kernel(x_ref, g_ref, w_ref, c64_ref, s64_ref, c32_ref, s32_ref,
                   qa_ref, kva_ref, qi_ref, kiw_ref, qbn_ref, qbr_ref, kvb_ref, cmp_ref, gsig_ref, gate_ref, *, seq):
    x = x_ref[...]
    ms = jnp.mean(x * x, axis=-1, keepdims=True)
    h = (x * lax.rsqrt(ms + RMS_EPS) * g_ref[...]).astype(BF16)
    c64, s64, c32, s32 = c64_ref[...], s64_ref[...], c32_ref[...], s32_ref[...]
    lane = lax.broadcasted_iota(I32, c64.shape, 1)

    def proj(lo, width):
        return jnp.dot(h, w_ref[:, lo:lo + width], preferred_element_type=F32)

    def slab(p, j):
        return p[:, j * LANES:(j + 1) * LANES]

    p = proj(P_QA, 512)
    for j in range(4):
        qa_ref[:, j * LANES:(j + 1) * LANES] = (_rope_slab(slab(p, j), c64, s64, 32) * SCALE).astype(BF16)
    p = proj(P_KVA, 128)
    kva_ref[:, :LANES] = jnp.where(lane < HEAD_DIM, _rope_slab(p, c64, s64, 32), p).astype(BF16)
    kva_ref[:, LANES:] = jnp.where(lane == 0, 1.0, jnp.where(lane < HEAD_DIM, 0.0, p)).astype(BF16)
    p = proj(P_QI, 256)
    for j in range(2):
        qi_ref[:, j * LANES:(j + 1) * LANES] = _rope_slab(slab(p, j), c32, s32, 16).astype(BF16)
    p = proj(P_KIW, 128)
    kiw_ref[...] = jnp.where(lane < IDX_DIM, _rope_slab(p, c32, s32, 16), p)
    p = proj(P_QB, 512)
    qbn_ref[...] = (p * SCALE).astype(BF16)
    for j in range(4):
        qbr_ref[:, j * LANES:(j + 1) * LANES] = (_rope_slab(slab(p, j), c64, s64, 32) * SCALE).astype(BF16)
    p = proj(P_KVB, 512)
    tm = x.shape[0]
    pos = (pl.program_id(0) * tm) % seq + lax.broadcasted_iota(I32, c64.shape, 0)
    blk_hot = jnp.where(lane - HEAD_DIM == lax.shift_right_logical(pos, SLC_LEN.bit_length() - 1), 1.0, 0.0)
    for k in range(B_KV_HEADS):
        slc, win = slab(p, 2 * k), slab(p, 2 * k + 1)
        base = 3 * k * LANES
        kvb_ref[:, base:base + LANES] = jnp.where(lane < HEAD_DIM, _rope_slab(slc, c64, s64, 32), blk_hot).astype(BF16)
        kvb_ref[:, base + LANES:base + 2 * LANES] = jnp.where(lane < HEAD_DIM, _rope_slab(win, c64, s64, 32), win).astype(BF16)
        kvb_ref[:, base + 2 * LANES:base + 3 * LANES] = jnp.where(
            lane == 0, 1.0, jnp.where(lane < HEAD_DIM, 0.0, slc)).astype(BF16)
    cmp_ref[...] = proj(P_CMP, 256)
    gsig_ref[...] = jax.nn.sigmoid(proj(P_GB, 128))
    for j in range(2):
        gate_ref[:, j * 1024:(j + 1) * 1024] = jax.nn.sigmoid(proj(P_GATE + j * 1024, 1024)).astype(BF16)


def _inproj(x2, g, w_p, tables, tm, seq):
    n = x2.shape[0]
    assert seq % tm == 0
    row = lambda w: pl.BlockSpec((tm, w), lambda i: (i, 0))
    full = lambda a: pl.BlockSpec(a.shape, lambda i: (0, 0))
    outs = [(512, BF16), (256, BF16), (256, BF16), (128, F32), (512, BF16), (512, BF16), (3 * LANES * B_KV_HEADS, BF16),
            (256, F32), (128, F32), (2048, BF16)]
    return pl.pallas_call(
        functools.partial(_inproj_kernel, seq=seq),
        grid=(n // tm,),
        in_specs=[row(D_MODEL), full(g), full(w_p)] + [row(LANES)] * 4,
        out_specs=[row(w) for w, _ in outs],
        out_shape=[jax.ShapeDtypeStruct((n, w), dt) for w, dt in outs],
        compiler_params=_cparams(("parallel",)),
        name="inproj",
    )(x2, g, w_p, *tables)


def _compress_kernel(g_ref, pe_ref, w1_ref, w2_ref, o_ref):
    g = g_ref[0, 0]
    half = CMP_STRIDE * HEAD_DIM
    a = jnp.dot((g + pe_ref[0, :, :half]).astype(BF16), w1_ref[0, :half, :], preferred_element_type=F32)
    b = jnp.dot((g + pe_ref[0, :, half:]).astype(BF16), w1_ref[0, half:, :], preferred_element_type=F32)
    n_grp = g.shape[0]
    hid = a + pltpu.roll(b, n_grp - 1, 0)
    hid = hid * jax.nn.sigmoid(hid)
    o_ref[0, 0] = jnp.dot(hid.astype(BF16), w2_ref[0], preferred_element_type=F32).astype(BF16)


def _compress(cmp4, pe, w1, w2):
    bsz, _, n_grp, width = cmp4.shape
    return pl.pallas_call(
        _compress_kernel,
        grid=(bsz, 4),
        in_specs=[pl.BlockSpec((1, 1, n_grp, width), lambda b, a: (b, a, 0, 0)),
                  pl.BlockSpec((1, 1, 2 * width), lambda b, a: (a // 2, 0, 0)),
                  pl.BlockSpec((1, 2 * width, CMP_HIDDEN), lambda b, a: (a // 2, 0, 0)),
                  pl.BlockSpec((1, CMP_HIDDEN, HEAD_DIM), lambda b, a: (a // 2, 0, 0))],
        out_specs=pl.BlockSpec((1, 1, n_grp, HEAD_DIM), lambda b, a: (b, a, 0, 0)),
        out_shape=jax.ShapeDtypeStruct((bsz, 4, n_grp, HEAD_DIM), BF16),
        compiler_params=_cparams(("parallel", "parallel")),
        name="compress",
    )(cmp4, pe, w1, w2)


def _reset(m_ref, acc_ref):
    m_ref[...] = jnp.full(m_ref.shape, M_INIT, F32)
    acc_ref[...] = jnp.zeros(acc_ref.shape, F32)


def _stack_heads_t(q, n_heads):
    t = q.shape[0]
    lane = lax.broadcasted_iota(I32, (t, LANES), 1)
    parts = []
    for h in range(n_heads):
        slab = q[:, (h // 2) * LANES:(h // 2 + 1) * LANES].astype(F32)
        if h % 2:
            slab = pltpu.roll(slab, HEAD_DIM, 1)
        parts.append(jnp.where(lane < HEAD_DIM, slab, 0.0).T)
    return jnp.concatenate(parts, axis=1).astype(BF16)


def _unstack_heads_t(o_t, n_heads, t):
    lane = lax.broadcasted_iota(I32, (t, LANES), 1)
    slabs = []
    for j in range(n_heads // 2):
        even = pltpu.roll(o_t[:, (2 * j) * t:(2 * j + 1) * t].T, HEAD_DIM, 1)
        slabs.append(jnp.where(lane < HEAD_DIM, even, o_t[:, (2 * j + 1) * t:(2 * j + 2) * t].T))
    return jnp.concatenate(slabs, axis=1)


def _tn_dot(a, b):
    return lax.dot_general(a, b, (((0,), (0,)), ((), ())), preferred_element_type=F32)


def _online_update_t(k_c, q_t, bias_t, v_c, n_heads, t, m_ref, acc_ref):
    s_t = jnp.dot(k_c, q_t, preferred_element_type=F32)
    ps, alphas = [], []
    for h in range(n_heads):
        cols = slice(h * t, (h + 1) * t)
        sh = s_t[:, cols]
        if bias_t is not None:
            sh = sh + bias_t
        m_old = m_ref[:, cols]
        m_new = jnp.maximum(m_old, jnp.max(sh, axis=0, keepdims=True))
        m_ref[:, cols] = m_new
        ps.append(jnp.exp2(sh - m_new).astype(BF16))
        alphas.append(jnp.exp2(m_old - m_new))
    acc_ref[...] = jnp.concatenate(alphas, axis=1) * acc_ref[...] + _tn_dot(v_c, jnp.concatenate(ps, axis=1))


INT_MIN = -2 ** 31


def _chunk_loop(n_keys, sizes, body, init):
    big, small = sizes
    n_big = n_keys // big
    carry = lax.fori_loop(0, n_big, lambda c, x: body(pl.multiple_of(c * big, big), big, x), init)
    rest = n_big * big
    n_small = (n_keys - rest + small - 1) // small
    return lax.fori_loop(0, n_small, lambda c, x: body(pl.multiple_of(rest + c * small, small), small, x), carry)


def _dsa_kernel(qa_ref, qi_ref, wq_ref, kva_ref, kiw_ref, o_ref,
                key_scr, m_scr, acc_scr, *, t, ck, k_sel):
    i = pl.program_id(1)
    t0 = i * t
    n_keys = t0 + t
    qpos = t0 + lax.broadcasted_iota(I32, (1, t), 1)

    def key_index(k0, size):
        return k0 + lax.broadcasted_iota(I32, (size, t), 0)

    qi_t = qi_ref[0].astype(F32).T
    w_t = wq_ref[0].T
    zpad = jnp.zeros((LANES - IDX_DIM, t), F32)
    qi_all = jnp.concatenate([jnp.concatenate([qi_t[h * IDX_DIM:(h + 1) * IDX_DIM], zpad], axis=0)
                              for h in range(IDX_HEADS)], axis=1).astype(BF16)

    def score_chunk(k0, size, carry):
        kic = kiw_ref[0, pl.ds(k0, size), :].astype(BF16)
        lg = jnp.dot(kic, qi_all, preferred_element_type=F32)
        acc = jnp.zeros((size, t), F32)
        for h in range(IDX_HEADS):
            acc = acc + w_t[IDX_DIM + h:IDX_DIM + h + 1] * jnp.maximum(lg[:, h * t:(h + 1) * t], 0.0)
        key_scr[pl.ds(k0, size), :] = jnp.where(key_index(k0, size) <= qpos, acc, -jnp.inf)
        return carry

    _chunk_loop(n_keys, ck, score_chunk, 0)

    n_part = 8

    def count_where(pred):
        def body(k0, size, cnt):
            hit = jnp.where(pred(key_scr[pl.ds(k0, size), :], k0, size), 1.0, 0.0)
            return cnt + jnp.sum(hit.reshape(n_part, size // (8 * n_part), 8, t), axis=1)
        cnt = _chunk_loop(n_keys, (ck[0], t), body, jnp.zeros((n_part, 8, t), F32))
        return jnp.sum(jnp.sum(cnt, axis=0), axis=0, keepdims=True)

    def as_score(code):
        return pltpu.bitcast(jnp.where(code < 0, code ^ 0x7FFFFFFF, code), F32)

    def bit_step(b, lo):
        cand = lo + lax.shift_left(jnp.int32(1), 31 - b)
        v = as_score(cand)
        return jnp.where(count_where(lambda sc, k0, size: sc >= v) >= k_sel, cand, lo)

    code = lax.fori_loop(0, 32, bit_step, jnp.full((1, t), INT_MIN, I32))
    thr = jnp.where(code == INT_MIN, -jnp.inf, as_score(code))
    n_gt = count_where(lambda sc, k0, size: sc > thr)
    n_ge = count_where(lambda sc, k0, size: sc >= thr)
    need = k_sel - n_gt

    def tie_bound():
        n_bits = int(key_scr.shape[0]).bit_length()

        def step(b, j):
            cand = j + lax.shift_left(jnp.int32(1), n_bits - 1 - b)
            below = count_where(lambda sc, k0, size: (sc == thr) & (key_index(k0, size) < cand))
            return jnp.where(below <= need, cand, j)
        return lax.fori_loop(0, n_bits, step, jnp.zeros((1, t), I32))

    has_excess = jnp.max(jnp.where(n_ge > k_sel, 1.0, 0.0)) > 0.0
    jbound = lax.cond(has_excess, tie_bound, lambda: jnp.full((1, t), 2 ** 30, I32))

    q_t = _stack_heads_t(qa_ref[0], A_HEADS)
    _reset(m_scr, acc_scr)

    def attn_chunk(k0, size, carry):
        key = key_scr[pl.ds(k0, size), :]
        idx = key_index(k0, size)
        sel = ((key > thr) | ((key == thr) & (idx < jbound))) & (idx <= qpos)
        bias_t = jnp.where(sel, 0.0, MASK_NEG)
        _online_update_t(kva_ref[0, pl.ds(k0, size), :LANES], q_t, bias_t, kva_ref[0, pl.ds(k0, size), LANES:],
                         A_HEADS, t, m_scr, acc_scr)
        return carry

    _chunk_loop(n_keys, ck, attn_chunk, 0)
    o_t = acc_scr[...] / acc_scr[0:1, :]
    o_ref[0] = _unstack_heads_t(o_t, A_HEADS, t).astype(BF16)


def _dsa(qa, qi, kiw, kva, t, ck):
    bsz, seq, _ = qa.shape
    k_sel = min(DSA_TOPK_MAX, seq // 4)
    kern = functools.partial(_dsa_kernel, t=t, ck=ck, k_sel=k_sel)
    blk = lambda w: pl.BlockSpec((1, t, w), lambda b, i: (b, i, 0))
    res = lambda w: pl.BlockSpec((1, seq, w), lambda b, i: (b, 0, 0))
    cols = A_HEADS * t
    return pl.pallas_call(
        kern,
        grid=(bsz, seq // t),
        in_specs=[blk(512), blk(256), blk(LANES), res(2 * LANES), res(LANES)],
        out_specs=blk(512),
        out_shape=jax.ShapeDtypeStruct((bsz, seq, 512), BF16),
        scratch_shapes=[pltpu.VMEM((seq, t), F32), pltpu.VMEM((1, cols), F32), pltpu.VMEM((LANES, cols), F32)],
        compiler_params=_cparams(("parallel", "arbitrary")),
        name="dsa",
    )(qa, qi, kiw, kva, kiw)


def _nsa_kernel(qn_ref, qr_ref, g_ref, kcv_ref, kv_ref, c2s_ref, o_ref,
                m_scr, acc_scr, *, t, ck, n_slc, n_sel):
    kh = pl.program_id(1)
    i = pl.program_id(2)
    t0 = i * t
    qpos = t0 + lax.broadcasted_iota(I32, (1, t), 1)
    qn_t = _stack_heads_t(qn_ref[0], B_GROUP)
    qr_t = _stack_heads_t(qr_ref[0], B_GROUP)
    heads = [slice(g * t, (g + 1) * t) for g in range(B_GROUP)]

    kcv = kcv_ref[0, 0]
    n_cp = kcv.shape[0]
    s_t = jnp.dot(kcv, qn_t, preferred_element_type=F32)
    cmp_end = lax.broadcasted_iota(I32, (n_cp, t), 0) * CMP_STRIDE + (CMP_LEN - 1)
    vis = cmp_end <= qpos
    probs = []
    for cols in heads:
        sh = jnp.where(vis, s_t[:, cols], M_INIT)
        p = jnp.where(vis, jnp.exp2(sh - jnp.max(sh, axis=0, keepdims=True)), 0.0)
        den = jnp.sum(p, axis=0, keepdims=True)
        probs.append(p * jnp.where(den > 0.0, 1.0 / den, 0.0))
    o_cmp_t = _tn_dot(kcv, jnp.concatenate(probs, axis=1).astype(BF16))

    psum = probs[0]
    for g in range(1, B_GROUP):
        psum = psum + probs[g]
    p_hi = psum.astype(BF16)
    p_lo = (psum - p_hi.astype(F32)).astype(BF16)
    imp = (jnp.dot(c2s_ref[...], p_hi, preferred_element_type=F32)
           + jnp.dot(c2s_ref[...], p_lo, preferred_element_type=F32))
    jx = lax.broadcasted_iota(I32, (n_slc, t), 0)
    admiss = jx * SLC_LEN <= qpos
    cur = lax.shift_right_logical(qpos, SLC_LEN.bit_length() - 1)
    forced = admiss & ((jx == 0) | (jx == cur) | (jx == cur - 1))
    score = jnp.where(forced, jnp.inf, jnp.where(admiss, imp, -jnp.inf))
    rank = jnp.zeros((n_slc, t), F32)
    for j2 in range(n_slc):
        row = score[j2:j2 + 1, :]
        ahead = (row > score) | ((row == score) & (j2 < jx))
        rank = rank + jnp.where(ahead, 1.0, 0.0)
    selbias = jnp.where(rank < n_sel, 0.0, MASK_NEG)
    if n_slc < HEAD_DIM:
        selbias = jnp.concatenate([selbias, jnp.zeros((HEAD_DIM - n_slc, t), F32)], axis=0)
    q_aug_t = jnp.concatenate([qr_t[:HEAD_DIM], jnp.concatenate([selbias.astype(BF16)] * B_GROUP, axis=1)], axis=0)

    _reset(m_scr, acc_scr)
    def slc_chunk(k0, size, causal):
        bias_t = None
        if causal:
            bias_t = jnp.where(k0 + lax.broadcasted_iota(I32, (size, t), 0) <= qpos, 0.0, MASK_NEG)
        _online_update_t(kv_ref[0, pl.ds(k0, size), :LANES], q_aug_t, bias_t, kv_ref[0, pl.ds(k0, size), 2 * LANES:],
                         B_GROUP, t, m_scr, acc_scr)
        return 0

    diag = ck[1]
    k_diag = pl.multiple_of(t0 // diag * diag, diag)
    _chunk_loop(k_diag, ck, lambda k0, size, carry: slc_chunk(k0, size, False), 0)
    slc_chunk(k_diag, diag, True)
    o_slc_t = acc_scr[...] / acc_scr[0:1, :]

    ww = WINDOW + t
    w0 = pl.multiple_of(jnp.maximum(t0 - WINDOW, 0), t)
    kvw = kv_ref[0, pl.ds(w0, ww), LANES:2 * LANES]
    s_w = jnp.dot(kvw, qr_t, preferred_element_type=F32)
    rel = qpos - (w0 + lax.broadcasted_iota(I32, (ww, t), 0))
    bias_w = jnp.where((rel >= 0) & (rel < WINDOW), 0.0, MASK_NEG)
    pws, dens = [], []
    for cols in heads:
        sh = s_w[:, cols] + bias_w
        p = jnp.exp2(sh - jnp.max(sh, axis=0, keepdims=True))
        dens.append(jnp.sum(p, axis=0, keepdims=True))
        pws.append(p.astype(BF16))
    o_win_t = _tn_dot(kvw, jnp.concatenate(pws, axis=1)) / jnp.concatenate(dens, axis=1)

    g_t = g_ref[0].T
    grow = lax.broadcasted_iota(I32, g_t.shape, 0)
    outs = []
    for g, cols in enumerate(heads):
        hd = kh * B_GROUP + g

        def gate(branch):
            return jnp.sum(jnp.where(grow == branch * B_HEADS + hd, g_t, 0.0), axis=0, keepdims=True)

        outs.append(gate(0) * o_cmp_t[:, cols] + gate(1) * o_slc_t[:, cols] + gate(2) * o_win_t[:, cols])
    o_ref[0] = _unstack_heads_t(jnp.concatenate(outs, axis=1), B_GROUP, t).astype(BF16)


def _nsa(qbn, qbr, gsig, kcv, kvb, t, ck):
    bsz, seq, _ = qbn.shape
    n_cp = kcv.shape[2]
    n_slc = seq // SLC_LEN
    n_sel = min(SLC_TOPN, n_slc)
    assert n_slc <= HEAD_DIM and seq >= WINDOW + t and ck[1] % t == 0 and ck[0] % ck[1] == 0
    n = np.arange(n_cp)
    js = np.arange(n_slc) * SLC_LEN
    overlap = (n[None, :] * CMP_STRIDE < js[:, None] + SLC_LEN) & (n[None, :] * CMP_STRIDE + CMP_LEN > js[:, None])
    overlap &= (n[None, :] < n_cp - 1)
    c2s_t = jnp.asarray(overlap.astype(np.float32), BF16)
    kern = functools.partial(_nsa_kernel, t=t, ck=ck, n_slc=n_slc, n_sel=n_sel)
    cols = B_GROUP * t
    width = B_GROUP * HEAD_DIM
    qblk = pl.BlockSpec((1, t, width), lambda b, k, i: (b, i, k))
    return pl.pallas_call(
        kern,
        grid=(bsz, B_KV_HEADS, seq // t),
        in_specs=[qblk, qblk,
                  pl.BlockSpec((1, t, LANES), lambda b, k, i: (b, i, 0)),
                  pl.BlockSpec((1, 1, n_cp, LANES), lambda b, k, i: (b, k, 0, 0)),
                  pl.BlockSpec((1, seq, 3 * LANES), lambda b, k, i: (b, 0, k)),
                  pl.BlockSpec((n_slc, n_cp), lambda b, k, i: (0, 0))],
        out_specs=qblk,
        out_shape=jax.ShapeDtypeStruct((bsz, seq, B_HEADS * HEAD_DIM), BF16),
        scratch_shapes=[pltpu.VMEM((1, cols), F32), pltpu.VMEM((LANES, cols), F32)],
        compiler_params=_cparams(("parallel", "parallel", "arbitrary")),
        name="nsa",
    )(qbn, qbr, gsig, kcv, kvb, c2s_t)


R_EXP = 0
R_GRP = N_EXPERTS


def _merge_kernel(oa_ref, ob_ref, gate_ref, x_ref, wa_ref, wb_ref, wo_ref, g_ref, wr_ref, br_ref,
                  x1_ref, h2_ref, route_ref):
    ya = jnp.dot(oa_ref[...], wa_ref[...], preferred_element_type=F32)
    yb = jnp.dot(ob_ref[...], wb_ref[...], preferred_element_type=F32)
    merged = gate_ref[:, :D_MODEL].astype(F32) * ya + gate_ref[:, D_MODEL:].astype(F32) * yb
    x1 = x_ref[...] + jnp.dot(merged.astype(BF16), wo_ref[...], preferred_element_type=F32)
    x1_ref[...] = x1
    ms = jnp.mean(x1 * x1, axis=-1, keepdims=True)
    h2 = x1 * lax.rsqrt(ms + RMS_EPS) * g_ref[...]
    _to_tiles(h2_ref, h2)

    h_hi = h2.astype(BF16)
    h_lo = (h2 - h_hi.astype(F32)).astype(BF16)
    hi_both = jnp.dot(h_hi, wr_ref[...], preferred_element_type=F32)
    logit = (hi_both[:, :LANES] + hi_both[:, LANES:]
             + jnp.dot(h_lo, wr_ref[:, :LANES], preferred_element_type=F32) + br_ref[...])
    lane = lax.broadcasted_iota(I32, logit.shape, 1)
    lane_f = lane.astype(F32)

    def first_argmax(v):
        mx = jnp.max(v, axis=-1, keepdims=True)
        return mx, jnp.min(jnp.where(v == mx, lane_f, float(LANES)), axis=-1, keepdims=True)

    is_grp = (lane >= R_GRP) & (lane < R_GRP + N_GROUPS)
    gl = jnp.where(is_grp, logit, -jnp.inf)
    gmax, garg = first_argmax(gl)
    g_sel = (garg - R_GRP).astype(I32)
    g_prob = 1.0 / jnp.sum(jnp.where(is_grp, jnp.exp(gl - gmax), 0.0), axis=-1, keepdims=True)
    grp_of_lane = lax.shift_right_logical(lane, EXPERTS_PER_GROUP.bit_length() - 1)
    in_grp = (lane < N_EXPERTS) & (grp_of_lane == g_sel)
    el = jnp.where(in_grp, logit, -jnp.inf)
    v1, i1 = first_argmax(el)
    v2, i2 = first_argmax(jnp.where(lane_f == i1, -jnp.inf, el))
    e2 = jnp.exp(v2 - v1)
    w1 = g_prob / (1.0 + e2)
    w2 = g_prob * e2 / (1.0 + e2)
    route_ref[...] = jnp.where(lane == 0, i1, jnp.where(lane == 1, i2,
                               jnp.where(lane == 2, w1, jnp.where(lane == 3, w2, 0.0))))


def _merge(o_a, o_b, gates, x2, wa, wb, wo, g, wr, br, tm):
    n = x2.shape[0]
    row = lambda w: pl.BlockSpec((tm, w), lambda i: (i, 0))
    full = lambda a: pl.BlockSpec(a.shape, lambda i: (0, 0))
    return pl.pallas_call(
        _merge_kernel,
        grid=(n // tm,),
        in_specs=[row(512), row(512), row(2048), row(D_MODEL), full(wa), full(wb), full(wo), full(g), full(wr), full(br)],
        out_specs=[row(D_MODEL), pl.BlockSpec((tm,) + TILE_ROW, lambda i: (i, 0, 0)), row(LANES)],
        out_shape=[jax.ShapeDtypeStruct((n, D_MODEL), F32), jax.ShapeDtypeStruct((n,) + TILE_ROW, F32),
                   jax.ShapeDtypeStruct((n, LANES), F32)],
        compiler_params=_cparams(("parallel",)),
        name="merge",
    )(o_a, o_b, gates, x2, wa, wb, wo, g, wr, br)


SUBLANES = 8
TILE_ROW = (SUBLANES, LANES)
assert SUBLANES * LANES == D_MODEL


def _to_tiles(ref, x):
    for s in range(SUBLANES):
        ref[:, s, :] = x[:, s * LANES:(s + 1) * LANES]


def _from_tiles(ref):
    return jnp.concatenate([ref[:, s, :] for s in range(SUBLANES)], axis=1)


EXPERT_LOOKAHEAD = 2
EXPERT_SLOTS = EXPERT_LOOKAHEAD + 1


def _expert_kernel(be_ref, slot_ref, h_hbm, wgu_ref, wd_ref, y_ref, buf, sem, wgu_bf, wd_bf):
    i = pl.program_id(0)

    def gather(block, slot):
        base = block * MOE_BLOCK

        def issue(r, carry):
            pltpu.make_async_copy(h_hbm.at[slot_ref[base + r]], buf.at[slot, r], sem.at[slot]).start()
            return carry

        lax.fori_loop(0, MOE_BLOCK, issue, 0, unroll=8)

    @pl.when(i == 0)
    def _():
        for b in range(EXPERT_LOOKAHEAD):
            gather(b, b)

    @pl.when(i + EXPERT_LOOKAHEAD < pl.num_programs(0))
    def _():
        gather(i + EXPERT_LOOKAHEAD, (i + EXPERT_LOOKAHEAD) % EXPERT_SLOTS)

    @pl.when((i == 0) | (be_ref[i] != be_ref[jnp.maximum(i - 1, 0)]))
    def _():
        wgu_bf[...] = wgu_ref[0].astype(BF16)
        wd_bf[...] = wd_ref[0].astype(BF16)

    slot = i % EXPERT_SLOTS
    pltpu.make_async_copy(h_hbm.at[pl.ds(0, MOE_BLOCK)], buf.at[slot], sem.at[slot]).wait()
    xb = _from_tiles(buf.at[slot]).astype(BF16)
    gu = jnp.dot(xb, wgu_bf[...], preferred_element_type=F32)
    gate, up = gu[:, :D_EXPERT], gu[:, D_EXPERT:]
    act = (gate * jax.nn.sigmoid(gate) * up).astype(BF16)
    _to_tiles(y_ref, jnp.dot(act, wd_bf[...], preferred_element_type=F32))


def _experts(blk_expert, slot_tok, h2, wgu, wd):
    n_blocks = blk_expert.shape[0]
    assert n_blocks > EXPERT_LOOKAHEAD
    grid_spec = pltpu.PrefetchScalarGridSpec(
        num_scalar_prefetch=2,
        grid=(n_blocks,),
        in_specs=[pl.BlockSpec(memory_space=pl.ANY),
                  pl.BlockSpec((1, D_MODEL, 2 * D_EXPERT), lambda i, be, st: (be[i], 0, 0)),
                  pl.BlockSpec((1, D_EXPERT, D_MODEL), lambda i, be, st: (be[i], 0, 0))],
        out_specs=pl.BlockSpec((MOE_BLOCK,) + TILE_ROW, lambda i, be, st: (i, 0, 0)),
        scratch_shapes=[pltpu.VMEM((EXPERT_SLOTS, MOE_BLOCK) + TILE_ROW, F32), pltpu.SemaphoreType.DMA((EXPERT_SLOTS,)),
                        pltpu.VMEM((D_MODEL, 2 * D_EXPERT), BF16), pltpu.VMEM((D_EXPERT, D_MODEL), BF16)],
    )
    return pl.pallas_call(
        _expert_kernel,
        grid_spec=grid_spec,
        out_shape=jax.ShapeDtypeStruct((n_blocks * MOE_BLOCK,) + TILE_ROW, F32),
        compiler_params=_cparams(("arbitrary",)),
        name="experts",
    )(blk_expert, slot_tok, h2, wgu, wd)


def _combine_kernel(dest_ref, y_hbm, x1_ref, route_ref, g_ref, o_ref, buf, sem, *, tm):
    i = pl.program_id(0)

    def gather(tile, slot):
        base = tile * tm * 2

        def issue(r, carry):
            for k in range(2):
                pltpu.make_async_copy(y_hbm.at[dest_ref[base + 2 * r + k]],
                                      buf.at[slot, k * tm + r], sem.at[slot]).start()
            return carry

        lax.fori_loop(0, tm, issue, 0, unroll=4)

    @pl.when(i == 0)
    def _():
        gather(0, 0)

    @pl.when(i + 1 < pl.num_programs(0))
    def _():
        gather(i + 1, (i + 1) % 2)

    slot = i % 2
    pltpu.make_async_copy(y_hbm.at[pl.ds(0, 2 * tm)], buf.at[slot], sem.at[slot]).wait()
    route = route_ref[...]
    x = (x1_ref[...] + route[:, 2:3] * _from_tiles(buf.at[slot, pl.ds(0, tm)])
         + route[:, 3:4] * _from_tiles(buf.at[slot, pl.ds(tm, tm)]))
    ms = jnp.mean(x * x, axis=-1, keepdims=True)
    o_ref[...] = x * lax.rsqrt(ms + RMS_EPS) * g_ref[...]


def _combine(dest, y_buf, x1, route, g, tm):
    n = x1.shape[0]
    grid_spec = pltpu.PrefetchScalarGridSpec(
        num_scalar_prefetch=1,
        grid=(n // tm,),
        in_specs=[pl.BlockSpec(memory_space=pl.ANY),
                  pl.BlockSpec((tm, D_MODEL), lambda i, d: (i, 0)),
                  pl.BlockSpec((tm, LANES), lambda i, d: (i, 0)),
                  pl.BlockSpec((1, D_MODEL), lambda i, d: (0, 0))],
        out_specs=pl.BlockSpec((tm, D_MODEL), lambda i, d: (i, 0)),
        scratch_shapes=[pltpu.VMEM((2, 2 * tm) + TILE_ROW, F32), pltpu.SemaphoreType.DMA((2,))],
    )
    return pl.pallas_call(
        functools.partial(_combine_kernel, tm=tm),
        grid_spec=grid_spec,
        out_shape=jax.ShapeDtypeStruct((n, D_MODEL), F32),
        compiler_params=_cparams(("arbitrary",)),
        name="combine",
    )(dest, y_buf, x1, route, g)


def _dispatch_plan(expert):
    n_tok = expert.shape[0]
    n_asg = n_tok * 2
    e_flat = expert.reshape(n_asg)
    onehot = (e_flat[:, None] == jnp.arange(N_EXPERTS, dtype=I32)[None, :]).astype(I32)
    csum = jnp.cumsum(onehot, axis=0)
    rank = jnp.take_along_axis(csum, e_flat[:, None], axis=1)[:, 0] - 1
    counts = csum[-1]
    padded = (counts + MOE_BLOCK - 1) // MOE_BLOCK * MOE_BLOCK
    pad_end = jnp.cumsum(padded)
    pad_start = pad_end - padded
    dest = (pad_start[e_flat] + rank).astype(I32)
    n_blocks = -(-n_asg // MOE_BLOCK) + N_EXPERTS
    slot_tok = jnp.zeros((n_blocks * MOE_BLOCK,), I32).at[dest].set(jnp.arange(n_asg, dtype=I32) // 2)
    blk_start = jnp.arange(n_blocks, dtype=I32) * MOE_BLOCK
    blk_expert = jnp.minimum(jnp.sum((pad_end[None, :] <= blk_start[:, None]).astype(I32), axis=1), N_EXPERTS - 1)
    return dest, slot_tok, blk_expert


def _pick(n, prefs):
    for p in prefs:
        if n % p == 0:
            return p
    return n


def _layer(x, positions, norm_mix, w_in, pe_k, w1_k, w2_k, pe_v, w1_v, w2_v, w_br_a, w_br_b, w_out,
           norm_ffn, w_group, b_group, w_expert, b_expert, w_gate_up, w_down, out_gain):
    bsz, seq, d = x.shape
    n = bsz * seq
    x2 = x.reshape(n, d)
    tables = _rope_tables(positions)

    w_p = _permute_columns(w_in.astype(BF16))
    tm = _pick(n, (512, 256, 128))
    qa, kva, qi, kiw, qbn, qbr, kvb, cmp_, gsig, gates = _inproj(x2, norm_mix[None, :], w_p, tables, tm, seq)

    n_grp = seq // CMP_STRIDE
    cmp4 = cmp_.reshape(bsz, seq, 4, HEAD_DIM).transpose(0, 2, 1, 3).reshape(bsz, 4, n_grp, CMP_STRIDE * HEAD_DIM)
    pe = jnp.stack([pe_k.reshape(1, -1), pe_v.reshape(1, -1)])
    kcvc = _compress(cmp4, pe, jnp.stack([w1_k, w1_v]).astype(BF16), jnp.stack([w2_k, w2_v]).astype(BF16))
    kcv = jnp.concatenate([kcvc[:, :B_KV_HEADS], kcvc[:, B_KV_HEADS:]], axis=-1)

    r3 = lambda a: a.reshape(bsz, seq, a.shape[-1])
    t = _pick(seq, (256, 128))
    ck = (_pick(seq, (1024, 512)), _pick(seq, (512,)))
    o_a = _dsa(r3(qa), r3(qi), r3(kiw), r3(kva), t, ck)
    o_b = _nsa(r3(qbn), r3(qbr), r3(gsig), kcv, r3(kvb), _pick(seq, (512, 256, 128)), ck)

    wr = jnp.concatenate([w_expert, w_group, jnp.zeros((d, LANES - N_EXPERTS - N_GROUPS), F32)], axis=1)
    wr_hi = wr.astype(BF16)
    wr = jnp.concatenate([wr_hi, (wr - wr_hi.astype(F32)).astype(BF16)], axis=1)
    br = jnp.zeros((1, LANES), F32).at[0, R_EXP:R_EXP + N_EXPERTS].set(b_expert).at[0, R_GRP:R_GRP + N_GROUPS].set(b_group)
    x1, h2, route = _merge(o_a.reshape(n, 512), o_b.reshape(n, 512), gates, x2,
                           w_br_a.astype(BF16), w_br_b.astype(BF16), w_out.astype(BF16),
                           norm_ffn[None, :], wr, br, _pick(n, (512, 256, 128)))

    expert = route[:, :2].astype(I32)
    dest, slot_tok, blk_expert = _dispatch_plan(expert)
    y_buf = _experts(blk_expert, slot_tok, h2, w_gate_up, w_down)
    out = _combine(dest, y_buf, x1, route, out_gain[None, :], _pick(n, (256, 128)))
    return out.reshape(bsz, seq, d)


def kernel(x, positions, norm_mix, w_in, pe_k, w1_k, w2_k, pe_v, w1_v, w2_v, w_br_a, w_br_b, w_out, norm_ffn,
           w_group, b_group, w_expert, b_expert, w_gate_up, w_down, norm_final):
    assert norm_mix.shape[0] == 1, "single-layer block"
    l = 0
    return _layer(x, positions, norm_mix[l], w_in[l], pe_k[l], w1_k[l], w2_k[l], pe_v[l], w1_v[l], w2_v[l],
                  w_br_a[l], w_br_b[l], w_out[l], norm_ffn[l], w_group[l], b_group[l], w_expert[l], b_expert[l],
                  w_gate_up[l], w_down[l], norm_final)
```
